```python
import functools
import jax
import jax.numpy as jnp
from jax import lax
import numpy as np

D_MODEL = 2048
BATCH = 2
SEQ = 4096
DEPTH = 2
DEC_BATCH = 128
DEC_SEQ = 1
PAST_LEN = 2048
PAGE_SIZE = 128

HEAD_DIM = 128
MIX_WIDTH = D_MODEL
A_WIDTH = MIX_WIDTH // 2
A_HEADS = A_WIDTH // HEAD_DIM
B_WIDTH = MIX_WIDTH - A_WIDTH
B_GROUPS = B_WIDTH // HEAD_DIM
CHUNK = 128
C_HEADS = MIX_WIDTH // (2 * HEAD_DIM)
C_WIDTH = C_HEADS * HEAD_DIM
D_WIDTH = MIX_WIDTH - C_WIDTH
CONV_A = 4
CONV_D = 31
LRU_C = 8.0
Q_BLOCK = 128
N_MEM = 256
X_HEADS = 4
X_HEAD_DIM = 128
X_WIDTH = X_HEADS * X_HEAD_DIM
D_FF = 5632
N_EXPERTS = 8
TOP_K = 2
D_FF_EXPERT = 5632
NORM_EPS = 1e-6
ATTN_SCALE = HEAD_DIM ** -0.5
X_SCALE = X_HEAD_DIM ** -0.5
F32 = jnp.float32

kernel_name = 'hybrid_rglru_chunkmlp_fox_conformer_decode_step'


def _lin_combine(left, right):
    a_l, b_l = left
    a_r, b_r = right
    return a_l * a_r, a_r * b_l + b_r


def rmsnorm(x, g):
    xf = x.astype(F32)
    y = xf * lax.rsqrt(jnp.mean(xf * xf, axis=-1, keepdims=True) + NORM_EPS)
    return (y * g.astype(F32)).astype(x.dtype)


def layernorm(x, g, b):
    xf = x.astype(F32)
    xc = xf - jnp.mean(xf, axis=-1, keepdims=True)
    var = jnp.mean(xc * xc, axis=-1, keepdims=True)
    return (xc * lax.rsqrt(var + NORM_EPS) * g.astype(F32) + b.astype(F32)).astype(x.dtype)


def causal_dwconv(x, buf, w, b):
    width = w.shape[0]
    xp = jnp.concatenate([buf.astype(x.dtype), x], axis=1)
    y = lax.conv_general_dilated(xp, w[:, None, :].astype(x.dtype), window_strides=(1,), padding='VALID',
                                 dimension_numbers=('NWC', 'WIO', 'NWC'), feature_group_count=x.shape[-1])
    return y + b.astype(x.dtype), xp[:, xp.shape[1] - (width - 1):]


def rg_lru(xc, h0, w_ra, b_ra, w_ri, b_ri, lam):
    n, t, _ = xc.shape
    xh = xc.reshape(n, t, A_HEADS, HEAD_DIM)
    r = jax.nn.sigmoid(jnp.einsum('nthi,hij->nthj', xh, w_ra).reshape(n, t, A_WIDTH) + b_ra)
    i = jax.nn.sigmoid(jnp.einsum('nthi,hij->nthj', xh, w_ri).reshape(n, t, A_WIDTH) + b_ri)
    log_a = LRU_C * r.astype(F32) * jax.nn.log_sigmoid(lam.astype(F32))
    a = jnp.exp(log_a)
    bx = jnp.sqrt(-jnp.expm1(2.0 * log_a)) * (i * xc).astype(F32)
    bx = bx.at[:, 0].add(a[:, 0] * h0.astype(F32))
    _, h = lax.associative_scan(_lin_combine, (a, bx), axis=1)
    return h


def chunk_spatial_gate(u, v, w_s, b_s):
    n, t, _ = v.shape
    length = min(t, CHUNK)
    vc = v.reshape(n, t // length, length, B_GROUPS, HEAD_DIM)
    causal = jnp.tril(jnp.ones((length, length), dtype=bool))
    ws = jnp.where(causal, w_s[:, :length, :length], 0).astype(v.dtype)
    s = jnp.einsum('gts,ncsgd->nctgd', ws, vc) + b_s[:, :length].T[None, None, :, :, None].astype(v.dtype)
    return u * s.reshape(n, t, B_WIDTH)


def mixer_ab(h, lru_h0, lru_conv0, p):
    z = h @ p['w_in0']
    gate, xr, u, v = jnp.split(z, [A_WIDTH, 2 * A_WIDTH, 2 * A_WIDTH + B_WIDTH], axis=-1)
    xc, lru_conv = causal_dwconv(xr, lru_conv0, p['w_conv_a'], p['b_conv_a'])
    hs = rg_lru(xc, lru_h0, p['w_rg_a'], p['b_rg_a'], p['w_rg_i'], p['b_rg_i'], p['lru_lambda'])
    ya = jax.nn.gelu(gate) * hs.astype(h.dtype)
    u = jax.nn.gelu(u)
    v = layernorm(jax.nn.gelu(v), p['ln_v_g'], p['ln_v_b'])
    yb = chunk_spatial_gate(u, v, p['w_spatial'], p['b_spatial'])
    y = jnp.concatenate([ya, yb], axis=-1) @ p['w_out0']
    return y, hs[:, -1].astype(h.dtype), lru_conv, v


def fox_prompt(q, k, v, logf):
    b, t, nh, hd = q.shape
    c = jnp.cumsum(logf, axis=1)
    ct = c.transpose(0, 2, 1)
    nblk = t // Q_BLOCK
    qb = q.reshape(b, nblk, Q_BLOCK, nh, hd).transpose(1, 0, 2, 3, 4)
    cb = c.reshape(b, nblk, Q_BLOCK, nh).transpose(1, 0, 2, 3)
    pos_k = jnp.arange(t)

    def block(args):
        qi, ci, idx = args
        s = jnp.einsum('bqhd,bkhd->bhqk', qi, k).astype(F32) * ATTN_SCALE
        s = s + ci.transpose(0, 2, 1)[..., None] - ct[:, :, None, :]
        pos_q = idx * Q_BLOCK + jnp.arange(Q_BLOCK)
        s = jnp.where(pos_k[None, :] <= pos_q[:, None], s, -jnp.inf)
        pr = jax.nn.softmax(s, axis=-1).astype(v.dtype)
        return jnp.einsum('bhqk,bkhd->bqhd', pr, v)

    o = lax.map(block, (qb, cb, jnp.arange(nblk)))
    return o.transpose(1, 0, 2, 3, 4).reshape(b, t, nh * hd)


def fox_sample(cache_k, cache_v, cache_logf, page_table, q, k, v, logf):
    n, t, nh, hd = q.shape
    past = page_table.shape[1] * PAGE_SIZE
    kp = cache_k[page_table].reshape(n, past, nh, hd).astype(q.dtype)
    vp = cache_v[page_table].reshape(n, past, nh, hd).astype(v.dtype)
    lp = cache_logf[page_table].reshape(n, past, nh).astype(F32)
    c = jnp.cumsum(jnp.concatenate([lp, logf], axis=1), axis=1)
    ct = c.transpose(0, 2, 1)
    s_past = jnp.einsum('nqhd,nkhd->nhqk', q, kp).astype(F32)
    s_new = jnp.einsum('nqhd,nkhd->nhqk', q, k).astype(F32)
    s = jnp.concatenate([s_past, s_new], axis=-1) * ATTN_SCALE
    s = s + ct[:, :, past:, None] - ct[:, :, None, :]
    pos_k = jnp.arange(past + t)
    pos_q = past + jnp.arange(t)
    s = jnp.where(pos_k[None, :] <= pos_q[:, None], s, -jnp.inf)
    pr = jax.nn.softmax(s, axis=-1).astype(v.dtype)
    o = jnp.einsum('nhqk,nkhd->nqhd', pr[..., :past], vp) + jnp.einsum('nhqk,nkhd->nqhd', pr[..., past:], v)
    return o.reshape(n, t, nh * hd)


def mixer_cd(h, conv_d0, attend, p):
    n, t, _ = h.shape
    z = h @ p['w_in1']
    q, k, v, fl, gd = jnp.split(z, [C_WIDTH, 2 * C_WIDTH, 3 * C_WIDTH, 3 * C_WIDTH + C_HEADS], axis=-1)
    q = q.reshape(n, t, C_HEADS, HEAD_DIM)
    k = k.reshape(n, t, C_HEADS, HEAD_DIM)
    v = v.reshape(n, t, C_HEADS, HEAD_DIM)
    logf = jax.nn.log_sigmoid((fl + p['b_forget']).astype(F32))
    yc = attend(q, k, v, logf).astype(h.dtype)
    ga, gb = jnp.split(gd, 2, axis=-1)
    glu = ga * jax.nn.sigmoid(gb)
    dc, conv_d = causal_dwconv(glu, conv_d0, p['w_conv_d'], p['b_conv_d'])
    yd = jax.nn.silu(layernorm(dc, p['ln_d_g'], p['ln_d_b']))
    y = jnp.concatenate([yc, yd], axis=-1) @ p['w_out1']
    return y, conv_d, k, v, logf


def memory_kv(mem, g, w_k, w_v):
    n, m, _ = mem.shape
    mn = rmsnorm(mem, g)
    return (mn @ w_k).reshape(n, m, X_HEADS, X_HEAD_DIM), (mn @ w_v).reshape(n, m, X_HEADS, X_HEAD_DIM)


def cross_attn(h, mk, mv, w_q, w_o):
    n, t, _ = h.shape
    q = (h @ w_q).reshape(n, t, X_HEADS, X_HEAD_DIM)
    s = jnp.einsum('nthd,nmhd->nhtm', q, mk.astype(h.dtype)).astype(F32) * X_SCALE
    pr = jax.nn.softmax(s, axis=-1).astype(h.dtype)
    o = jnp.einsum('nhtm,nmhd->nthd', pr, mv.astype(h.dtype)).reshape(n, t, X_WIDTH)
    return o @ w_o


def swiglu(h, wg, wu, wd):
    return (jax.nn.silu(h @ wg) * (h @ wu)) @ wd


def moe_swiglu(h, w_router, wg, wu, wd):
    n, t, d = h.shape
    hf = h.reshape(n * t, d)
    logits = (hf @ w_router).astype(F32)
    top_v, top_i = lax.top_k(logits, TOP_K)
    gates = jax.nn.softmax(top_v, axis=-1)
    combine = jnp.sum(jax.nn.one_hot(top_i, N_EXPERTS, dtype=F32) * gates[..., None], axis=1).astype(h.dtype)
    out = jnp.zeros_like(hf)
    for e in range(N_EXPERTS):
        out = out + combine[:, e:e + 1] * swiglu(hf, wg[e], wu[e], wd[e])
    return out.reshape(n, t, d)


def trunk(x, mem_k, mem_v, lru_h0, lru_conv0, conv_d0, attend, p):
    for layer in range(DEPTH):
        h = rmsnorm(x, p['norm_mix'][layer])
        if layer % 2 == 0:
            y, lru_h, lru_conv, chunk_v = mixer_ab(h, lru_h0, lru_conv0, p)
        else:
            y, conv_d, k_rows, v_rows, logf_rows = mixer_cd(h, conv_d0, attend, p)
        x = x + y
        x = x + cross_attn(rmsnorm(x, p['norm_cross'][layer]), mem_k[layer], mem_v[layer],
                           p['w_xq'][layer], p['w_xo'][layer])
        h = rmsnorm(x, p['norm_ffn'][layer])
        if layer % 2 == 0:
            x = x + swiglu(h, p['w_ffn_gate'], p['w_ffn_up'], p['w_ffn_down'])
        else:
            x = x + moe_swiglu(h, p['w_router'], p['w_exp_gate'], p['w_exp_up'], p['w_exp_down'])
    y = rmsnorm(x, p['norm_final'])
    return y, lru_h, lru_conv, chunk_v, conv_d, k_rows, v_rows, logf_rows


def setup_inputs(seed: int = 0) -> dict:
    key = jax.random.key(seed)
    ks = iter(jax.random.split(key, 64))

    def nrm(shape, scale=1.0):
        return scale * jax.random.normal(next(ks), shape, F32)

    def gain(shape):
        return 1.0 + nrm(shape, 0.05)

    d = D_MODEL
    n_pages = PAST_LEN // PAGE_SIZE
    n_used = DEC_BATCH * n_pages
    n_pool = n_used + max(1, n_used // 4)
    page_table = jax.random.permutation(next(ks), n_pool)[:n_used].reshape(DEC_BATCH, n_pages).astype(jnp.int32)
    u = jax.random.uniform(next(ks), (A_WIDTH,), F32, 0.9, 0.999)
    s = u ** (1.0 / LRU_C)
    lru_lambda = jnp.log(s) - jnp.log1p(-s)
    return {
        'x_prompt': nrm((BATCH, SEQ, d)),
        'x_sample': nrm((DEC_BATCH, DEC_SEQ, d)),
        'mem_prompt': nrm((BATCH, N_MEM, d)),
        'state_lru_h': nrm((DEC_BATCH, A_WIDTH), 0.5),
        'state_lru_conv': nrm((DEC_BATCH, CONV_A - 1, A_WIDTH)),
        'cache_fox_k': nrm((n_pool, PAGE_SIZE, C_HEADS, HEAD_DIM)),
        'cache_fox_v': nrm((n_pool, PAGE_SIZE, C_HEADS, HEAD_DIM)),
        'cache_fox_logf': jax.nn.log_sigmoid(3.0 + nrm((n_pool, PAGE_SIZE, C_HEADS))),
        'state_conv_d': nrm((DEC_BATCH, CONV_D - 1, D_WIDTH)),
        'cache_mem_k': nrm((DEPTH, DEC_BATCH, N_MEM, X_HEADS, X_HEAD_DIM)),
        'cache_mem_v': nrm((DEPTH, DEC_BATCH, N_MEM, X_HEADS, X_HEAD_DIM)),
        'page_table': page_table,
        'norm_mix': gain((DEPTH, d)),
        'norm_cross': gain((DEPTH, d)),
        'norm_mem': gain((DEPTH, d)),
        'norm_ffn': gain((DEPTH, d)),
        'norm_final': gain((d,)),
        'w_in0': nrm((d, 2 * A_WIDTH + 2 * B_WIDTH), d ** -0.5),
        'w_conv_a': nrm((CONV_A, A_WIDTH), CONV_A ** -0.5),
        'b_conv_a': nrm((A_WIDTH,), 0.1),
        'w_rg_a': nrm((A_HEADS, HEAD_DIM, HEAD_DIM), HEAD_DIM ** -0.5),
        'b_rg_a': nrm((A_WIDTH,), 0.1),
        'w_rg_i': nrm((A_HEADS, HEAD_DIM, HEAD_DIM), HEAD_DIM ** -0.5),
        'b_rg_i': nrm((A_WIDTH,), 0.1),
        'lru_lambda': lru_lambda,
        'ln_v_g': gain((B_WIDTH,)),
        'ln_v_b': nrm((B_WIDTH,), 0.1),
        'w_spatial': nrm((B_GROUPS, CHUNK, CHUNK), CHUNK ** -0.5),
        'b_spatial': gain((B_GROUPS, CHUNK)),
        'w_out0': nrm((MIX_WIDTH, d), MIX_WIDTH ** -0.5),
        'w_in1': nrm((d, 3 * C_WIDTH + C_HEADS + 2 * D_WIDTH), d ** -0.5),
        'b_forget': 3.0 + nrm((C_HEADS,), 0.1),
        'w_conv_d': nrm((CONV_D, D_WIDTH), CONV_D ** -0.5),
        'b_conv_d': nrm((D_WIDTH,), 0.1),
        'ln_d_g': gain((D_WIDTH,)),
        'ln_d_b': nrm((D_WIDTH,), 0.1),
        'w_out1': nrm((MIX_WIDTH, d), MIX_WIDTH ** -0.5),
        'w_xq': nrm((DEPTH, d, X_WIDTH), d ** -0.5),
        'w_xk': nrm((DEPTH, d, X_WIDTH), d ** -0.5),
        'w_xv': nrm((DEPTH, d, X_WIDTH), d ** -0.5),
        'w_xo': nrm((DEPTH, X_WIDTH, d), X_WIDTH ** -0.5),
        'w_ffn_gate': nrm((d, D_FF), d ** -0.5),
        'w_ffn_up': nrm((d, D_FF), d ** -0.5),
        'w_ffn_down': nrm((D_FF, d), D_FF ** -0.5),
        'w_router': nrm((d, N_EXPERTS), d ** -0.5),
        'w_exp_gate': nrm((N_EXPERTS, d, D_FF_EXPERT), d ** -0.5),
        'w_exp_up': nrm((N_EXPERTS, d, D_FF_EXPERT), d ** -0.5),
        'w_exp_down': nrm((N_EXPERTS, D_FF_EXPERT, d), D_FF_EXPERT ** -0.5),
    }


def reference(x_prompt, x_sample, mem_prompt, state_lru_h, state_lru_conv, cache_fox_k, cache_fox_v,
              cache_fox_logf, state_conv_d, cache_mem_k, cache_mem_v, page_table,
              norm_mix, norm_cross, norm_mem, norm_ffn, norm_final,
              w_in0, w_conv_a, b_conv_a, w_rg_a, b_rg_a, w_rg_i, b_rg_i, lru_lambda, ln_v_g, ln_v_b,
              w_spatial, b_spatial, w_out0,
              w_in1, b_forget, w_conv_d, b_conv_d, ln_d_g, ln_d_b, w_out1,
              w_xq, w_xk, w_xv, w_xo,
              w_ffn_gate, w_ffn_up, w_ffn_down,
              w_router, w_exp_gate, w_exp_up, w_exp_down):
    p = dict(norm_mix=norm_mix, norm_cross=norm_cross, norm_ffn=norm_ffn, norm_final=norm_final,
             w_in0=w_in0, w_conv_a=w_conv_a, b_conv_a=b_conv_a, w_rg_a=w_rg_a, b_rg_a=b_rg_a,
             w_rg_i=w_rg_i, b_rg_i=b_rg_i, lru_lambda=lru_lambda, ln_v_g=ln_v_g, ln_v_b=ln_v_b,
             w_spatial=w_spatial, b_spatial=b_spatial, w_out0=w_out0,
             w_in1=w_in1, b_forget=b_forget, w_conv_d=w_conv_d, b_conv_d=b_conv_d,
             ln_d_g=ln_d_g, ln_d_b=ln_d_b, w_out1=w_out1, w_xq=w_xq, w_xo=w_xo,
             w_ffn_gate=w_ffn_gate, w_ffn_up=w_ffn_up, w_ffn_down=w_ffn_down,
             w_router=w_router, w_exp_gate=w_exp_gate, w_exp_up=w_exp_up, w_exp_down=w_exp_down)

    n_p = x_prompt.shape[0]
    mem_kv_p = [memory_kv(mem_prompt, norm_mem[layer], w_xk[layer], w_xv[layer]) for layer in range(DEPTH)]
    mem_k_p = jnp.stack([kv[0] for kv in mem_kv_p], axis=0)
    mem_v_p = jnp.stack([kv[1] for kv in mem_kv_p], axis=0)
    (y_prompt, p_lru_h, p_lru_conv, _, p_conv_d, p_k, p_v, p_logf) = trunk(
        x_prompt, mem_k_p, mem_v_p,
        jnp.zeros((n_p, A_WIDTH), x_prompt.dtype),
        jnp.zeros((n_p, CONV_A - 1, A_WIDTH), x_prompt.dtype),
        jnp.zeros((n_p, CONV_D - 1, D_WIDTH), x_prompt.dtype),
        fox_prompt, p)

    attend_s = functools.partial(fox_sample, cache_fox_k, cache_fox_v, cache_fox_logf, page_table)
    (y_sample, s_lru_h, s_lru_conv, s_chunk_v, s_conv_d, s_k, s_v, s_logf) = trunk(
        x_sample, cache_mem_k, cache_mem_v, state_lru_h, state_lru_conv, state_conv_d, attend_s, p)

    return (y_prompt, y_sample,
            p_lru_h, p_lru_conv, p_k, p_v, p_logf, p_conv_d, mem_k_p, mem_v_p,
            s_lru_h, s_lru_conv, s_chunk_v, s_k, s_v, s_logf, s_conv_d)
```

```python
import functools

import jax
import jax.numpy as jnp
from jax import lax
from jax.experimental import pallas as pl
from jax.experimental.pallas import tpu as pltpu

F32 = jnp.float32
BF16 = jnp.bfloat16

LANES = 128
HEAD_DIM = 128
CHUNK = 128
LRU_C = 8.0
NORM_EPS = 1e-6
VMEM_LIMIT = 56 << 20


def _cp(*sem, vmem=VMEM_LIMIT):
    return pltpu.CompilerParams(dimension_semantics=sem, vmem_limit_bytes=vmem)


def _tile(n, pref, mult=8):
    if n <= pref:
        return n
    t = (pref // mult) * mult
    while t >= mult:
        if n % t == 0:
            return t
        t -= mult
    return n


def _gelu(x):
    return x * (0.5 * (1.0 + jnp.tanh(0.7978845608028654 * (x + 0.044715 * (x * x * x)))))


def _sigmoid(x):
    return 1.0 / (1.0 + jnp.exp(-x))


def _silu(x):
    return x * _sigmoid(x)


def _log_sigmoid(x):
    return jnp.minimum(x, 0.0) - jnp.log1p(jnp.exp(-jnp.abs(x)))


def _neg_expm1(y):
    return -jnp.tanh(0.5 * y) * (jnp.exp(y) + 1.0)


def _rmsnorm(x, g):
    return x * lax.rsqrt(jnp.mean(x * x, axis=-1, keepdims=True) + NORM_EPS) * g


def _layernorm(x, g, b):
    xc = x - jnp.mean(x, axis=-1, keepdims=True)
    var = jnp.mean(xc * xc, axis=-1, keepdims=True)
    return xc * lax.rsqrt(var + NORM_EPS) * g + b


def _dot(a, b):
    return jnp.dot(a, b, preferred_element_type=F32)


def _dot_t(a, b):
    return lax.dot_general(a, b, (((1,), (1,)), ((), ())), preferred_element_type=F32)


def _nmm_kernel(x_ref, g_ref, w_ref, o_ref, h_ref, *, norm, exact):
    @pl.when(pl.program_id(1) == 0)
    def _():
        x = x_ref[...]
        if norm:
            x = _rmsnorm(x, g_ref[...])
        h_ref[...] = x.astype(h_ref.dtype)

    if exact:
        o_ref[...] = jnp.dot(h_ref[...], w_ref[...], preferred_element_type=F32,
                             precision=lax.Precision.HIGHEST)
    else:
        o_ref[...] = _dot(h_ref[...], w_ref[...].astype(BF16))


def norm_matmul(x, g, w, *, tm=1024, tn=512, norm=True, exact=False):
    m, k = x.shape
    n = w.shape[1]
    tm, tn = _tile(m, tm), _tile(n, tn, LANES)
    return pl.pallas_call(
        functools.partial(_nmm_kernel, norm=norm, exact=exact),
        grid=(m // tm, n // tn),
        in_specs=[pl.BlockSpec((tm, k), lambda i, j: (i, 0)),
                  pl.BlockSpec((1, k), lambda i, j: (0, 0)),
                  pl.BlockSpec((k, tn), lambda i, j: (0, j))],
        out_specs=pl.BlockSpec((tm, tn), lambda i, j: (i, j)),
        out_shape=jax.ShapeDtypeStruct((m, n), F32),
        scratch_shapes=[pltpu.VMEM((tm, k), F32 if exact else BF16)],
        compiler_params=_cp("parallel", "arbitrary"),
        name="norm_matmul",
    )(x, g.reshape(1, k), w)


def _mmres_kernel(*refs, n_in):
    a_refs, w_refs, r_ref, o_ref = refs[:n_in], refs[n_in:2 * n_in], refs[2 * n_in], refs[2 * n_in + 1]
    acc = r_ref[...]
    for a_ref, w_ref in zip(a_refs, w_refs):
        acc = acc + _dot(a_ref[...].astype(BF16), w_ref[...].astype(BF16))
    o_ref[...] = acc


def matmul_residual(parts, w, res, *, tm=1024, tn=512):
    m, n = res.shape
    tm, tn = _tile(m, tm), _tile(n, tn, LANES)
    kp = parts[0].shape[1]
    assert all(p.shape[1] == kp for p in parts) and kp * len(parts) == w.shape[0]
    n_in = len(parts)
    in_specs = [pl.BlockSpec((tm, kp), lambda i, j: (i, 0)) for _ in parts]
    in_specs += [pl.BlockSpec((kp, tn), functools.partial(lambda i, j, c: (c, j), c=c)) for c in range(n_in)]
    in_specs += [pl.BlockSpec((tm, tn), lambda i, j: (i, j))]
    return pl.pallas_call(
        functools.partial(_mmres_kernel, n_in=n_in),
        grid=(m // tm, n // tn),
        in_specs=in_specs,
        out_specs=pl.BlockSpec((tm, tn), lambda i, j: (i, j)),
        out_shape=jax.ShapeDtypeStruct((m, n), F32),
        compiler_params=_cp("parallel", "arbitrary"),
        name="matmul_residual",
    )(*parts, *([w] * n_in), res)


def _lru_prompt_kernel(gate_ref, xr_ref, wc_ref, bc_ref, wra_ref, bra_ref, wri_ref, bri_ref, lam_ref,
                       ya_ref, hl_ref, xl_ref, xp_ref, *, t, width):
    pad = 8
    xr = xr_ref[...]
    xp_ref[0:pad, :] = jnp.zeros((pad, LANES), F32)
    xp_ref[pad:pad + t, :] = xr
    xc = bc_ref[0]
    for j in range(width):
        xc = xc + wc_ref[0, j:j + 1, :] * xp_ref[pl.ds(pad - (width - 1) + j, t), :]
    xcb = xc.astype(BF16)
    r = _sigmoid(_dot(xcb, wra_ref[0].astype(BF16)) + bra_ref[0])
    ig = _sigmoid(_dot(xcb, wri_ref[0].astype(BF16)) + bri_ref[0])
    log_a = LRU_C * r * _log_sigmoid(lam_ref[0])
    a = jnp.exp(log_a)
    b = jnp.sqrt(_neg_expm1(2.0 * log_a)) * (ig * xc)
    row = lax.broadcasted_iota(jnp.int32, (t, LANES), 0)
    s = 1
    while s < t:
        keep = row >= s
        a_sh = jnp.where(keep, pltpu.roll(a, s, 0), 1.0)
        b_sh = jnp.where(keep, pltpu.roll(b, s, 0), 0.0)
        b = b + a * b_sh
        a = a * a_sh
        s *= 2
    ya_ref[...] = (_gelu(gate_ref[...]) * b).astype(ya_ref.dtype)
    hl_ref[0] = b[t - 8:t, :]
    xl_ref[0] = xr[t - 8:t, :]


def lru_prompt(z, n_seq, t, wc, bc, wra, bra, wri, bri, lam):
    a_width = wc.shape[1]
    nh = a_width // LANES
    width = wc.shape[0]
    wc_h = wc.reshape(width, nh, LANES).transpose(1, 0, 2)
    vec = lambda v: v.reshape(nh, 1, LANES)
    vspec = pl.BlockSpec((1, 1, LANES), lambda n, h: (h, 0, 0))
    mspec = pl.BlockSpec((1, LANES, LANES), lambda n, h: (h, 0, 0))
    return pl.pallas_call(
        functools.partial(_lru_prompt_kernel, t=t, width=width),
        grid=(n_seq, nh),
        in_specs=[pl.BlockSpec((t, LANES), lambda n, h: (n, h)),
                  pl.BlockSpec((t, LANES), lambda n, h, nh=nh: (n, nh + h)),
                  pl.BlockSpec((1, width, LANES), lambda n, h: (h, 0, 0)), vspec,
                  mspec, vspec, mspec, vspec, vspec],
        out_specs=[pl.BlockSpec((t, LANES), lambda n, h: (n, h)),
                   pl.BlockSpec((1, 8, LANES), lambda n, h: (n, 0, h)),
                   pl.BlockSpec((1, 8, LANES), lambda n, h: (n, 0, h))],
        out_shape=[jax.ShapeDtypeStruct((n_seq * t, a_width), BF16),
                   jax.ShapeDtypeStruct((n_seq, 8, a_width), F32),
                   jax.ShapeDtypeStruct((n_seq, 8, a_width), F32)],
        scratch_shapes=[pltpu.VMEM((t + 8, LANES), F32)],
        compiler_params=_cp("parallel", "parallel"),
        name="lru_prompt",
    )(z, z, wc_h, vec(bc), wra, vec(bra), wri, vec(bri), vec(lam))


def _sgu_prompt_kernel(u_ref, v_ref, lng_ref, lnb_ref, ws_ref, bst_ref, yb_ref, *, tt, ng):
    vn = _layernorm(_gelu(v_ref[...]), lng_ref[...], lnb_ref[...])
    ug = _gelu(u_ref[...])
    r = lax.broadcasted_iota(jnp.int32, (CHUNK, CHUNK), 0)
    c = lax.broadcasted_iota(jnp.int32, (CHUNK, CHUNK), 1)
    causal = c <= r
    bst = bst_ref[...]
    for g in range(ng):
        wg = jnp.where(causal, ws_ref[g], 0.0).astype(BF16)
        bias = bst[:, g:g + 1]
        for ch in range(tt // CHUNK):
            rows = slice(ch * CHUNK, (ch + 1) * CHUNK)
            cols = slice(g * LANES, (g + 1) * LANES)
            s = _dot(wg, vn[rows, cols].astype(BF16)) + bias
            yb_ref[rows, cols] = (ug[rows, cols] * s).astype(yb_ref.dtype)


def sgu_prompt(z, n_seq, t, a_width, lng, lnb, ws, bs, *, tt=512):
    m = n_seq * t
    b_width = lng.shape[0]
    ng = b_width // LANES
    tt = _tile(t, tt, CHUNK)
    ucol, vcol = (2 * a_width) // b_width, (2 * a_width) // b_width + 1
    return pl.pallas_call(
        functools.partial(_sgu_prompt_kernel, tt=tt, ng=ng),
        grid=(m // tt,),
        in_specs=[pl.BlockSpec((tt, b_width), lambda i, c=ucol: (i, c)),
                  pl.BlockSpec((tt, b_width), lambda i, c=vcol: (i, c)),
                  pl.BlockSpec((1, b_width), lambda i: (0, 0)),
                  pl.BlockSpec((1, b_width), lambda i: (0, 0)),
                  pl.BlockSpec((ng, CHUNK, CHUNK), lambda i: (0, 0, 0)),
                  pl.BlockSpec((CHUNK, ng), lambda i: (0, 0))],
        out_specs=pl.BlockSpec((tt, b_width), lambda i: (i, 0)),
        out_shape=jax.ShapeDtypeStruct((m, b_width), BF16),
        compiler_params=_cp("parallel"),
        name="sgu_prompt",
    )(z, z, lng.reshape(1, -1), lnb.reshape(1, -1), ws, bs.T)


def _mixer0_sample_kernel(z_ref, h0_ref, cv_ref, wc_ref, bc_ref, wra_ref, bra_ref, wri_ref, bri_ref, lam_ref,
                          lng_ref, lnb_ref, ws0_ref, bs0_ref, yab_ref, h_ref, v_ref, *, a_width, width):
    aw = a_width
    gate, xr, u, v = (z_ref[:, k * aw:(k + 1) * aw] for k in range(4))
    xc = bc_ref[...] + wc_ref[width - 1:width, :] * xr
    for j in range(width - 1):
        xc = xc + wc_ref[j:j + 1, :] * cv_ref[:, j * aw:(j + 1) * aw]
    xcb = xc.astype(BF16)
    nh = aw // LANES
    rs, gs = [], []
    for h in range(nh):
        xh = xcb[:, h * LANES:(h + 1) * LANES]
        rs.append(_dot(xh, wra_ref[h].astype(BF16)))
        gs.append(_dot(xh, wri_ref[h].astype(BF16)))
    r = _sigmoid(jnp.concatenate(rs, axis=1) + bra_ref[...])
    ig = _sigmoid(jnp.concatenate(gs, axis=1) + bri_ref[...])
    log_a = LRU_C * r * _log_sigmoid(lam_ref[...])
    hs = jnp.exp(log_a) * h0_ref[...] + jnp.sqrt(_neg_expm1(2.0 * log_a)) * (ig * xc)
    h_ref[...] = hs
    vn = _layernorm(_gelu(v), lng_ref[...], lnb_ref[...])
    v_ref[...] = vn
    yab_ref[:, 0:aw] = _gelu(gate) * hs
    yab_ref[:, aw:2 * aw] = _gelu(u) * (ws0_ref[...] * vn + bs0_ref[...])


def mixer0_sample(z, h0, conv0, wc, bc, wra, bra, wri, bri, lam, lng, lnb, ws, bs):
    n, a_width = h0.shape
    width = wc.shape[0]
    row = lambda v: v.reshape(1, -1)
    ws0 = jnp.repeat(ws[:, 0, 0], LANES).reshape(1, -1)
    bs0 = jnp.repeat(bs[:, 0], LANES).reshape(1, -1)
    args = (z, h0, conv0.reshape(n, -1), wc, row(bc), wra, row(bra), wri, row(bri), row(lam),
            row(lng), row(lnb), ws0, bs0)
    return pl.pallas_call(
        functools.partial(_mixer0_sample_kernel, a_width=a_width, width=width),
        out_shape=[jax.ShapeDtypeStruct((n, 2 * a_width), F32),
                   jax.ShapeDtypeStruct((n, a_width), F32),
                   jax.ShapeDtypeStruct((n, a_width), F32)],
        compiler_params=pltpu.CompilerParams(vmem_limit_bytes=VMEM_LIMIT),
        name="mixer0_sample",
    )(*args)


def _logf_kernel(fl_ref, b_ref, lf_ref, c_ref, *, t):
    lf = _log_sigmoid(fl_ref[0] + b_ref[...])
    lf_ref[0] = lf
    lane = lax.broadcasted_iota(jnp.int32, lf.shape, 1)
    c = lf
    s = 1
    while s < t:
        c = c + jnp.where(lane >= s, pltpu.roll(c, s, 1), 0.0)
        s *= 2
    c_ref[0] = c


def logf_cumsum(fl_t, b_forget):
    n, h, t = fl_t.shape
    spec = pl.BlockSpec((1, h, t), lambda i: (i, 0, 0))
    return pl.pallas_call(
        functools.partial(_logf_kernel, t=t),
        grid=(n,),
        in_specs=[spec, pl.BlockSpec((h, 1), lambda i: (0, 0))],
        out_specs=[spec, spec],
        out_shape=[jax.ShapeDtypeStruct((n, h, t), F32)] * 2,
        compiler_params=_cp("parallel"),
        name="logf_cumsum",
    )(fl_t, b_forget.reshape(h, 1))


def _fox_prompt_kernel(q_ref, k_ref, v_ref, ck_ref, o_ref, m_ref, l_ref, acc_ref, *, scale, tq, tk):
    qi, ki = pl.program_id(2), pl.program_id(3)

    @pl.when(ki == 0)
    def _():
        m_ref[...] = jnp.full(m_ref.shape, -jnp.inf, F32)
        l_ref[...] = jnp.zeros(l_ref.shape, F32)
        acc_ref[...] = jnp.zeros(acc_ref.shape, F32)

    @pl.when(ki <= qi)
    def _():
        s = _dot_t(q_ref[...].astype(BF16), k_ref[...].astype(BF16)) * scale - ck_ref[0]
        rows = qi * tq + lax.broadcasted_iota(jnp.int32, (tq, tk), 0)
        cols = ki * tk + lax.broadcasted_iota(jnp.int32, (tq, tk), 1)
        s = jnp.where(cols <= rows, s, -jnp.inf)
        m_prev = m_ref[...]
        m_new = jnp.maximum(m_prev, jnp.max(s, axis=1, keepdims=True))
        alpha = jnp.exp(m_prev - m_new)
        p = jnp.exp(s - m_new)
        l_ref[...] = alpha * l_ref[...] + jnp.sum(p, axis=1, keepdims=True)
        acc_ref[...] = alpha * acc_ref[...] + _dot(p.astype(BF16), v_ref[...].astype(BF16))
        m_ref[...] = m_new

    @pl.when(ki == qi)
    def _():
        o_ref[...] = (acc_ref[...] / l_ref[...]).astype(o_ref.dtype)


def fox_prompt(qkv, c_t, n_seq, t, nh, *, tq=512):
    tq = _tile(t, tq, LANES)
    nq = t // tq
    return pl.pallas_call(
        functools.partial(_fox_prompt_kernel, scale=HEAD_DIM ** -0.5, tq=tq, tk=tq),
        grid=(n_seq, nh, nq, nq),
        in_specs=[pl.BlockSpec((tq, HEAD_DIM), lambda n, h, qi, ki: (n * nq + qi, h)),
                  pl.BlockSpec((tq, HEAD_DIM), lambda n, h, qi, ki: (n * nq + jnp.minimum(ki, qi), nh + h)),
                  pl.BlockSpec((tq, HEAD_DIM), lambda n, h, qi, ki: (n * nq + jnp.minimum(ki, qi), 2 * nh + h)),
                  pl.BlockSpec((1, 1, tq), lambda n, h, qi, ki: (n * nh + h, 0, jnp.minimum(ki, qi)))],
        out_specs=pl.BlockSpec((tq, HEAD_DIM), lambda n, h, qi, ki: (n * nq + qi, h)),
        out_shape=jax.ShapeDtypeStruct((n_seq * t, nh * HEAD_DIM), BF16),
        scratch_shapes=[pltpu.VMEM((tq, 1), F32), pltpu.VMEM((tq, 1), F32), pltpu.VMEM((tq, HEAD_DIM), F32)],
        compiler_params=_cp("parallel", "parallel", "parallel", "arbitrary"),
        name="fox_prompt",
    )(qkv, qkv, qkv, c_t)


def _fox_sample_kernel(pt_ref, q_ref, kn_ref, vn_ref, lfn_ref, kc_ref, vc_ref, lfc_ref, o_ref,
                       m_ref, l_ref, acc_ref, r_ref, *, scale, nh, n_pages, page):
    p_idx = pl.program_id(1)
    width = nh * HEAD_DIM
    hrow = lax.broadcasted_iota(jnp.int32, (nh, width), 0)
    hcol = lax.broadcasted_iota(jnp.int32, (nh, width), 1) // HEAD_DIM
    qe = jnp.where(hrow == hcol, jnp.broadcast_to(q_ref[0], (nh, width)), 0.0)

    @pl.when(p_idx == 0)
    def _():
        m_ref[...] = jnp.broadcast_to(jnp.sum(qe * kn_ref[0], axis=1, keepdims=True) * scale, m_ref.shape)
        l_ref[...] = jnp.ones(l_ref.shape, F32)
        acc_ref[...] = jnp.broadcast_to(vn_ref[0], (nh, width))
        r_ref[...] = lfn_ref[0]

    lf = lfc_ref[0]
    lane = lax.broadcasted_iota(jnp.int32, (nh, page), 1)
    suf = lf
    s = 1
    while s < page:
        suf = suf + jnp.where(lane + s < page, pltpu.roll(suf, page - s, 1), 0.0)
        s *= 2
    run = r_ref[...]
    logits = _dot_t(qe.astype(BF16), kc_ref[0].astype(BF16)) * scale + (run + suf - lf)
    r_ref[...] = run + suf[:, 0:1]
    m_prev = m_ref[:, 0:1]
    m_new = jnp.maximum(m_prev, jnp.max(logits, axis=1, keepdims=True))
    alpha = jnp.exp(m_prev - m_new)
    p = jnp.exp(logits - m_new)
    l_new = alpha * l_ref[:, 0:1] + jnp.sum(p, axis=1, keepdims=True)
    acc = alpha * acc_ref[...] + _dot(p.astype(BF16), vc_ref[0].astype(BF16))
    m_ref[...] = jnp.broadcast_to(m_new, m_ref.shape)
    l_ref[...] = jnp.broadcast_to(l_new, l_ref.shape)
    acc_ref[...] = acc

    @pl.when(p_idx == n_pages - 1)
    def _():
        out = acc / l_new
        o_ref[0] = jnp.concatenate([out[h:h + 1, h * HEAD_DIM:(h + 1) * HEAD_DIM] for h in range(nh)], axis=1)


def fox_sample(page_table, q, k_new, v_new, lf_new, cache_k, cache_v, cache_lf_t):
    n, width = q.shape
    nh = width // HEAD_DIM
    n_pages = page_table.shape[1]
    page = cache_k.shape[1]
    row3 = lambda a: a.reshape(n, 1, width)
    lf_b = jnp.broadcast_to(lf_new[:, :, None], (n, nh, LANES))
    rspec = pl.BlockSpec((1, 1, width), lambda i, p, pt: (i, 0, 0))
    pidx = lambda i, p, pt: (pt[i * n_pages + (n_pages - 1 - p)], 0, 0)
    grid_spec = pltpu.PrefetchScalarGridSpec(
        num_scalar_prefetch=1,
        grid=(n, n_pages),
        in_specs=[rspec, rspec, rspec,
                  pl.BlockSpec((1, nh, LANES), lambda i, p, pt: (i, 0, 0)),
                  pl.BlockSpec((1, page, width), pidx),
                  pl.BlockSpec((1, page, width), pidx),
                  pl.BlockSpec((1, nh, page), pidx)],
        out_specs=pl.BlockSpec((1, 1, width), lambda i, p, pt: (i, 0, 0)),
        scratch_shapes=[pltpu.VMEM((nh, LANES), F32), pltpu.VMEM((nh, LANES), F32),
                        pltpu.VMEM((nh, width), F32), pltpu.VMEM((nh, LANES), F32)],
    )
    out = pl.pallas_call(
        functools.partial(_fox_sample_kernel, scale=HEAD_DIM ** -0.5, nh=nh, n_pages=n_pages, page=page),
        grid_spec=grid_spec,
        out_shape=jax.ShapeDtypeStruct((n, 1, width), F32),
        compiler_params=_cp("parallel", "arbitrary"),
        name="fox_sample",
    )(page_table.reshape(-1), row3(q), row3(k_new), row3(v_new), lf_b, cache_k, cache_v, cache_lf_t)
    return out.reshape(n, width)


def _convd_prompt_kernel(ga_ref, gb_ref, wc_ref, bc_ref, lng_ref, lnb_ref, yd_ref, tail_ref, xp_ref, *, tt, width):
    pad = 32

    @pl.when(pl.program_id(1) == 0)
    def _():
        xp_ref[0:pad, :] = jnp.zeros((pad, xp_ref.shape[1]), F32)

    glu = ga_ref[...] * _sigmoid(gb_ref[...])
    xp_ref[pad:pad + tt, :] = glu
    dc = bc_ref[...]
    for j in range(width):
        dc = dc + wc_ref[j:j + 1, :] * xp_ref[pl.ds(pad - (width - 1) + j, tt), :]
    yd_ref[...] = _silu(_layernorm(dc, lng_ref[...], lnb_ref[...])).astype(yd_ref.dtype)
    tail_ref[0] = glu[tt - pad:tt, :]
    xp_ref[0:pad, :] = glu[tt - pad:tt, :]


def convd_prompt(gd, n_seq, t, wc, bc, lng, lnb, *, tt=512):
    width, d_width = wc.shape
    assert width - 1 <= 32
    tt = _tile(t, tt, 32)
    nt = t // tt
    row = lambda v: v.reshape(1, -1)
    cspec = pl.BlockSpec((1, d_width), lambda n, i: (0, 0))
    return pl.pallas_call(
        functools.partial(_convd_prompt_kernel, tt=tt, width=width),
        grid=(n_seq, nt),
        in_specs=[pl.BlockSpec((tt, d_width), lambda n, i: (n * nt + i, 0)),
                  pl.BlockSpec((tt, d_width), lambda n, i: (n * nt + i, 1)),
                  pl.BlockSpec((width, d_width), lambda n, i: (0, 0)), cspec, cspec, cspec],
        out_specs=[pl.BlockSpec((tt, d_width), lambda n, i: (n * nt + i, 0)),
                   pl.BlockSpec((1, 32, d_width), lambda n, i: (n, 0, 0))],
        out_shape=[jax.ShapeDtypeStruct((n_seq * t, d_width), BF16),
                   jax.ShapeDtypeStruct((n_seq, 32, d_width), F32)],
        scratch_shapes=[pltpu.VMEM((tt + 32, d_width), F32)],
        compiler_params=_cp("parallel", "arbitrary"),
        name="convd_prompt",
    )(gd, gd, wc, row(bc), row(lng), row(lnb))


def _convd_sample_kernel(gd_ref, st_ref, wc_ref, bc_ref, lng_ref, lnb_ref, yd_ref, glu_ref, *, width, d_width):
    glu = gd_ref[:, 0:d_width] * _sigmoid(gd_ref[:, d_width:2 * d_width])
    glu_ref[...] = glu
    dc = bc_ref[...] + wc_ref[width - 1:width, :] * glu
    for j in range(width - 1):
        dc = dc + wc_ref[j:j + 1, :] * st_ref[:, j, :]
    yd_ref[...] = _silu(_layernorm(dc, lng_ref[...], lnb_ref[...]))


def convd_sample(gd, state, wc, bc, lng, lnb, *, tb=16):
    n = gd.shape[0]
    width, d_width = wc.shape
    tb = _tile(n, tb)
    row = lambda v: v.reshape(1, -1)
    cspec = pl.BlockSpec((1, d_width), lambda i: (0, 0))
    return pl.pallas_call(
        functools.partial(_convd_sample_kernel, width=width, d_width=d_width),
        grid=(n // tb,),
        in_specs=[pl.BlockSpec((tb, 2 * d_width), lambda i: (i, 0)),
                  pl.BlockSpec((tb, width - 1, d_width), lambda i: (i, 0, 0)),
                  pl.BlockSpec((width, d_width), lambda i: (0, 0)), cspec, cspec, cspec],
        out_specs=[pl.BlockSpec((tb, d_width), lambda i: (i, 0)),
                   pl.BlockSpec((tb, d_width), lambda i: (i, 0))],
        out_shape=[jax.ShapeDtypeStruct((n, d_width), F32)] * 2,
        compiler_params=_cp("parallel"),
        name="convd_sample",
    )(gd, state, wc, row(bc), row(lng), row(lnb))


def _xattn_prompt_kernel(q_ref, k_ref, v_ref, o_ref, *, nh, scale):
    for h in range(nh):
        cols = slice(h * HEAD_DIM, (h + 1) * HEAD_DIM)
        s = _dot_t(q_ref[:, cols].astype(BF16), k_ref[0, :, cols].astype(BF16)) * scale
        e = jnp.exp(s - jnp.max(s, axis=1, keepdims=True))
        p = e / jnp.sum(e, axis=1, keepdims=True)
        o_ref[:, cols] = _dot(p.astype(BF16), v_ref[0, :, cols].astype(BF16)).astype(o_ref.dtype)


def xattn_prompt(q, mk, mv, n_seq, t, *, tt=512):
    xw = q.shape[1]
    n_mem = mk.shape[1]
    tt = _tile(t, tt, 16)
    nt = t // tt
    kspec = pl.BlockSpec((1, n_mem, xw), lambda n, i: (n, 0, 0))
    return pl.pallas_call(
        functools.partial(_xattn_prompt_kernel, nh=xw // HEAD_DIM, scale=HEAD_DIM ** -0.5),
        grid=(n_seq, nt),
        in_specs=[pl.BlockSpec((tt, xw), lambda n, i: (n * nt + i, 0)), kspec, kspec],
        out_specs=pl.BlockSpec((tt, xw), lambda n, i: (n * nt + i, 0)),
        out_shape=jax.ShapeDtypeStruct((n_seq * t, xw), BF16),
        compiler_params=_cp("parallel", "parallel"),
        name="xattn_prompt",
    )(q, mk, mv)


def _xattn_sample_kernel(q_ref, k_ref, v_ref, o_ref, *, nh, tb, scale):
    xw = nh * HEAD_DIM
    rows = 8
    hrow = lax.broadcasted_iota(jnp.int32, (rows, xw), 0)
    hcol = lax.broadcasted_iota(jnp.int32, (rows, xw), 1) // HEAD_DIM
    for b in range(tb):
        qe = jnp.where(hrow == hcol, jnp.broadcast_to(q_ref[b:b + 1, :], (rows, xw)), 0.0)
        s = _dot_t(qe.astype(BF16), k_ref[b].astype(BF16)) * scale
        e = jnp.exp(s - jnp.max(s, axis=1, keepdims=True))
        p = e / jnp.sum(e, axis=1, keepdims=True)
        o = _dot(p.astype(BF16), v_ref[b].astype(BF16))
        o_ref[b:b + 1, :] = jnp.concatenate(
            [o[h:h + 1, h * HEAD_DIM:(h + 1) * HEAD_DIM] for h in range(nh)], axis=1)


def xattn_sample(q, mk, mv, *, tb=8):
    n, xw = q.shape
    n_mem = mk.shape[1]
    tb = _tile(n, tb)
    kspec = pl.BlockSpec((tb, n_mem, xw), lambda i: (i, 0, 0))
    return pl.pallas_call(
        functools.partial(_xattn_sample_kernel, nh=xw // HEAD_DIM, tb=tb, scale=HEAD_DIM ** -0.5),
        grid=(n // tb,),
        in_specs=[pl.BlockSpec((tb, xw), lambda i: (i, 0)), kspec, kspec],
        out_specs=pl.BlockSpec((tb, xw), lambda i: (i, 0)),
        out_shape=jax.ShapeDtypeStruct((n, xw), F32),
        compiler_params=_cp("parallel"),
        name="xattn_sample",
    )(q, mk, mv)


def _ffn_kernel(x_ref, g_ref, wg_ref, wu_ref, wd_ref, o_ref, h_ref):
    j = pl.program_id(1)

    @pl.when(j == 0)
    def _():
        x = x_ref[...]
        h_ref[...] = _rmsnorm(x, g_ref[...]).astype(BF16)
        o_ref[...] = x

    h = h_ref[...]
    a = _silu(_dot(h, wg_ref[...].astype(BF16))) * _dot(h, wu_ref[...].astype(BF16))
    o_ref[...] += _dot(a.astype(BF16), wd_ref[...].astype(BF16))


def ffn_swiglu(x, g, wg, wu, wd, *, tm=1024, tf=256):
    m, d = x.shape
    f = wg.shape[1]
    tm, tf = _tile(m, tm), _tile(f, tf, LANES)
    return pl.pallas_call(
        _ffn_kernel,
        grid=(m // tm, f // tf),
        in_specs=[pl.BlockSpec((tm, d), lambda i, j: (i, 0)),
                  pl.BlockSpec((1, d), lambda i, j: (0, 0)),
                  pl.BlockSpec((d, tf), lambda i, j: (0, j)),
                  pl.BlockSpec((d, tf), lambda i, j: (0, j)),
                  pl.BlockSpec((tf, d), lambda i, j: (j, 0))],
        out_specs=pl.BlockSpec((tm, d), lambda i, j: (i, 0)),
        out_shape=jax.ShapeDtypeStruct((m, d), F32),
        scratch_shapes=[pltpu.VMEM((tm, d), BF16)],
        compiler_params=_cp("parallel", "arbitrary"),
        name="ffn_swiglu",
    )(x, g.reshape(1, d), wg, wu, wd)


def _top2_kernel(lg_ref, comb_ref, *, n_exp):
    lg = lg_ref[...]
    lane = lax.broadcasted_iota(jnp.int32, lg.shape, 1)
    neg = jnp.float32(-jnp.inf)
    big = jnp.int32(lg.shape[1])
    lg = jnp.where(lane < n_exp, lg, neg)
    m1 = jnp.max(lg, axis=1, keepdims=True)
    i1 = jnp.min(jnp.where(lg == m1, lane, big), axis=1, keepdims=True)
    rest = jnp.where(lane == i1, neg, lg)
    m2 = jnp.max(rest, axis=1, keepdims=True)
    i2 = jnp.min(jnp.where(rest == m2, lane, big), axis=1, keepdims=True)
    e2 = jnp.exp(m2 - m1)
    g1 = 1.0 / (1.0 + e2)
    g2 = e2 / (1.0 + e2)
    comb_ref[...] = jnp.where(lane == i1, g1, 0.0) + jnp.where(lane == i2, g2, 0.0)


def top2_combine(logits, n_exp, *, tm=512):
    m, w = logits.shape
    tm = _tile(m, tm)
    spec = pl.BlockSpec((tm, w), lambda i: (i, 0))
    return pl.pallas_call(
        functools.partial(_top2_kernel, n_exp=n_exp),
        grid=(m // tm,), in_specs=[spec], out_specs=spec,
        out_shape=jax.ShapeDtypeStruct((m, w), F32),
        compiler_params=_cp("parallel"),
        name="top2_combine",
    )(logits)


def _moe_kernel(x_ref, g_ref, comb_ref, wg_ref, wu_ref, wd_ref, gf_ref, o_ref, h_ref, acc_ref):
    e, j = pl.program_id(1), pl.program_id(2)

    @pl.when((e == 0) & (j == 0))
    def _():
        x = x_ref[...]
        h_ref[...] = _rmsnorm(x, g_ref[...]).astype(BF16)
        acc_ref[...] = jnp.zeros(acc_ref.shape, F32)

    h = h_ref[...]
    a = _silu(_dot(h, wg_ref[0].astype(BF16))) * _dot(h, wu_ref[0].astype(BF16))
    comb = comb_ref[...]
    lane = lax.broadcasted_iota(jnp.int32, comb.shape, 1)
    c = jnp.sum(jnp.where(lane == e, comb, 0.0), axis=1, keepdims=True)
    acc_ref[...] += c * _dot(a.astype(BF16), wd_ref[0].astype(BF16))

    @pl.when((e == pl.num_programs(1) - 1) & (j == pl.num_programs(2) - 1))
    def _():
        o_ref[...] = _rmsnorm(x_ref[...] + acc_ref[...], gf_ref[...])


def moe_dense(x, g, comb, wg, wu, wd, g_final, *, tm=512, tf=256):
    m, d = x.shape
    n_exp, _, f = wg.shape
    tm, tf = _tile(m, tm), _tile(f, tf, LANES)
    cw = comb.shape[1]
    return pl.pallas_call(
        _moe_kernel,
        grid=(m // tm, n_exp, f // tf),
        in_specs=[pl.BlockSpec((tm, d), lambda i, e, j: (i, 0)),
                  pl.BlockSpec((1, d), lambda i, e, j: (0, 0)),
                  pl.BlockSpec((tm, cw), lambda i, e, j: (i, 0)),
                  pl.BlockSpec((1, d, tf), lambda i, e, j: (e, 0, j)),
                  pl.BlockSpec((1, d, tf), lambda i, e, j: (e, 0, j)),
                  pl.BlockSpec((1, tf, d), lambda i, e, j: (e, j, 0)),
                  pl.BlockSpec((1, d), lambda i, e, j: (0, 0))],
        out_specs=pl.BlockSpec((tm, d), lambda i, e, j: (i, 0)),
        out_shape=jax.ShapeDtypeStruct((m, d), F32),
        scratch_shapes=[pltpu.VMEM((tm, d), BF16), pltpu.VMEM((tm, d), F32)],
        compiler_params=_cp("parallel", "arbitrary", "arbitrary"),
        name="moe_dense",
    )(x, g.reshape(1, d), comb, wg, wu, wd, g_final.reshape(1, d))


def _pad_cols(w, width=LANES):
    return jnp.pad(w, ((0, 0), (0, width - w.shape[1])))


def _cross_and_ffn0(x, p, mk, mv, xattn):
    q = norm_matmul(x, p['norm_cross'][0], p['w_xq'][0])
    x = matmul_residual([xattn(q, mk[0], mv[0])], p['w_xo'][0], x)
    return ffn_swiglu(x, p['norm_ffn'][0], p['w_ffn_gate'], p['w_ffn_up'], p['w_ffn_down'])


def _cross_and_moe(x, p, mk, mv, xattn):
    q = norm_matmul(x, p['norm_cross'][1], p['w_xq'][1])
    x = matmul_residual([xattn(q, mk[1], mv[1])], p['w_xo'][1], x)
    n_exp = p['w_router'].shape[1]
    logits = norm_matmul(x, p['norm_ffn'][1], _pad_cols(p['w_router']), exact=True)
    comb = top2_combine(logits, n_exp)
    return moe_dense(x, p['norm_ffn'][1], comb, p['w_exp_gate'], p['w_exp_up'], p['w_exp_down'], p['norm_final'])


def _split_in1(p, c_width, nh):
    w_main = jnp.concatenate([p['w_in1'][:, :3 * c_width], p['w_in1'][:, 3 * c_width + nh:]], axis=1)
    w_fl = _pad_cols(p['w_in1'][:, 3 * c_width:3 * c_width + nh])
    return w_main, w_fl


def trunk_prompt(x3, mem, p):
    n_seq, t, d = x3.shape
    x = x3.reshape(n_seq * t, d)
    a_width = p['w_conv_a'].shape[1]
    d_width = p['w_conv_d'].shape[1]
    nh = p['b_forget'].shape[0]
    c_width = nh * HEAD_DIM
    n_mem = mem.shape[1]
    xw = p['w_xk'].shape[2]

    mem2 = mem.reshape(n_seq * n_mem, d)
    mk, mv = [], []
    for layer in range(2):
        mk.append(norm_matmul(mem2, p['norm_mem'][layer], p['w_xk'][layer]).reshape(n_seq, n_mem, xw))
        mv.append(norm_matmul(mem2, p['norm_mem'][layer], p['w_xv'][layer]).reshape(n_seq, n_mem, xw))
    xattn = functools.partial(xattn_prompt, n_seq=n_seq, t=t)

    z = norm_matmul(x, p['norm_mix'][0], p['w_in0'])
    ya, h_last, xr_last = lru_prompt(z, n_seq, t, p['w_conv_a'], p['b_conv_a'], p['w_rg_a'], p['b_rg_a'],
                                     p['w_rg_i'], p['b_rg_i'], p['lru_lambda'])
    yb = sgu_prompt(z, n_seq, t, a_width, p['ln_v_g'], p['ln_v_b'], p['w_spatial'], p['b_spatial'])
    x = matmul_residual([ya, yb], p['w_out0'], x)
    x = _cross_and_ffn0(x, p, mk, mv, xattn)

    w_main, w_fl = _split_in1(p, c_width, nh)
    zz = norm_matmul(x, p['norm_mix'][1], w_main)
    qkv, gd = zz[:, :3 * c_width], zz[:, 3 * c_width:]
    fl = norm_matmul(x, p['norm_mix'][1], w_fl)[:, :nh]
    fl_t = fl.reshape(n_seq, t, nh).transpose(0, 2, 1)
    lf_t, c_t = logf_cumsum(fl_t, p['b_forget'])
    yc = fox_prompt(qkv, c_t.reshape(n_seq * nh, 1, t), n_seq, t, nh)
    yd, glu_tail = convd_prompt(gd, n_seq, t, p['w_conv_d'], p['b_conv_d'], p['ln_d_g'], p['ln_d_b'])
    x = matmul_residual([yc, yd], p['w_out1'], x)
    y = _cross_and_moe(x, p, mk, mv, xattn)

    width_a = p['w_conv_a'].shape[0]
    width_d = p['w_conv_d'].shape[0]
    return dict(
        y=y.reshape(n_seq, t, d),
        lru_h=h_last[:, 7, :],
        lru_conv=xr_last[:, 8 - (width_a - 1):, :],
        k=qkv[:, c_width:2 * c_width].reshape(n_seq, t, nh, HEAD_DIM),
        v=qkv[:, 2 * c_width:].reshape(n_seq, t, nh, HEAD_DIM),
        logf=lf_t.transpose(0, 2, 1),
        conv_d=glu_tail[:, 32 - (width_d - 1):, :],
        mem_k=jnp.stack(mk).reshape(2, n_seq, n_mem, xw // HEAD_DIM, HEAD_DIM),
        mem_v=jnp.stack(mv).reshape(2, n_seq, n_mem, xw // HEAD_DIM, HEAD_DIM),
    )


def trunk_sample(x3, mem_k, mem_v, lru_h0, lru_conv0, conv_d0, cache_k, cache_v, cache_lf, page_table, p):
    n, t, d = x3.shape
    assert t == 1
    x = x3.reshape(n, d)
    d_width = p['w_conv_d'].shape[1]
    nh = p['b_forget'].shape[0]
    c_width = nh * HEAD_DIM
    n_mem, xw = mem_k.shape[2], mem_k.shape[3] * mem_k.shape[4]
    mk = mem_k.reshape(2, n, n_mem, xw)
    mv = mem_v.reshape(2, n, n_mem, xw)

    z = norm_matmul(x, p['norm_mix'][0], p['w_in0'])
    a_width = lru_h0.shape[1]
    yab, lru_h, chunk_v = mixer0_sample(z, lru_h0, lru_conv0, p['w_conv_a'], p['b_conv_a'], p['w_rg_a'], p['b_rg_a'],
                                        p['w_rg_i'], p['b_rg_i'], p['lru_lambda'], p['ln_v_g'], p['ln_v_b'],
                                        p['w_spatial'], p['b_spatial'])
    x = matmul_residual([yab], p['w_out0'], x)
    x = _cross_and_ffn0(x, p, mk, mv, xattn_sample)

    w_main, w_fl = _split_in1(p, c_width, nh)
    zz = norm_matmul(x, p['norm_mix'][1], w_main)
    q, k, v = (zz[:, i * c_width:(i + 1) * c_width] for i in range(3))
    gd = zz[:, 3 * c_width:]
    fl = norm_matmul(x, p['norm_mix'][1], w_fl)[:, :nh]
    lf_t, _ = logf_cumsum(fl.T.reshape(1, nh, n), p['b_forget'])
    lf = lf_t.reshape(nh, n).T
    pool, page = cache_k.shape[0], cache_k.shape[1]
    yc = fox_sample(page_table, q, k, v, lf, cache_k.reshape(pool, page, c_width),
                    cache_v.reshape(pool, page, c_width), cache_lf.transpose(0, 2, 1))
    yd, glu = convd_sample(gd, conv_d0, p['w_conv_d'], p['b_conv_d'], p['ln_d_g'], p['ln_d_b'])
    x = matmul_residual([yc, yd], p['w_out1'], x)
    y = _cross_and_moe(x, p, mk, mv, xattn_sample)

    xr = z[:, a_width:2 * a_width]
    return dict(
        y=y.reshape(n, 1, d),
        lru_h=lru_h,
        lru_conv=jnp.concatenate([lru_conv0[:, 1:], xr[:, None, :]], axis=1),
        chunk_v=chunk_v.reshape(n, 1, -1),
        k=k.reshape(n, 1, nh, HEAD_DIM),
        v=v.reshape(n, 1, nh, HEAD_DIM),
        logf=lf.reshape(n, 1, nh),
        conv_d=jnp.concatenate([conv_d0[:, 1:], glu[:, None, :]], axis=1),
    )


def kernel(x_prompt, x_sample, mem_prompt, state_lru_h, state_lru_conv, cache_fox_k, cache_fox_v, cache_fox_logf, state_conv_d, cache_mem_k, cache_mem_v, page_table, norm_mix, norm_cross, norm_mem, norm_ffn, norm_final, w_in0, w_conv_a, b_conv_a, w_rg_a, b_rg_a, w_rg_i, b_rg_i, lru_lambda, ln_v_g, ln_v_b, w_spatial, b_spatial, w_out0, w_in1, b_forget, w_conv_d, b_conv_d, ln_d_g, ln_d_b, w_out1, w_xq, w_xk, w_xv, w_xo, w_ffn_gate, w_ffn_up, w_ffn_down, w_router, w_exp_gate, w_exp_up, w_exp_down):
    p = dict(norm_mix=norm_mix, norm_cross=norm_cross, norm_mem=norm_mem, norm_ffn=norm_ffn, norm_final=norm_final,
             w_in0=w_in0, w_conv_a=w_conv_a, b_conv_a=b_conv_a, w_rg_a=w_rg_a, b_rg_a=b_rg_a,
             w_rg_i=w_rg_i, b_rg_i=b_rg_i, lru_lambda=lru_lambda, ln_v_g=ln_v_g, ln_v_b=ln_v_b,
             w_spatial=w_spatial, b_spatial=b_spatial, w_out0=w_out0,
             w_in1=w_in1, b_forget=b_forget, w_conv_d=w_conv_d, b_conv_d=b_conv_d,
             ln_d_g=ln_d_g, ln_d_b=ln_d_b, w_out1=w_out1, w_xq=w_xq, w_xk=w_xk, w_xv=w_xv, w_xo=w_xo,
             w_ffn_gate=w_ffn_gate, w_ffn_up=w_ffn_up, w_ffn_down=w_ffn_down,
             w_router=w_router, w_exp_gate=w_exp_gate, w_exp_up=w_exp_up, w_exp_down=w_exp_down)
    pr = trunk_prompt(x_prompt, mem_prompt, p)
    sm = trunk_sample(x_sample, cache_mem_k, cache_mem_v, state_lru_h, state_lru_conv, state_conv_d,
                      cache_fox_k, cache_fox_v, cache_fox_logf, page_table, p)
    return (pr['y'], sm['y'],
            pr['lru_h'], pr['lru_conv'], pr['k'], pr['v'], pr['logf'], pr['conv_d'], pr['mem_k'], pr['mem_v'],
            sm['lru_h'], sm['lru_conv'], sm['chunk_v'], sm['k'], sm['v'], sm['logf'], sm['conv_d'])
```

```python
import functools

import jax
import jax.numpy as jnp
from jax import lax
from jax.experimental import pallas as pl
from jax.experimental.pallas import tpu as pltpu

F32 = jnp.float32
BF16 = jnp.bfloat16

LANES = 128
HEAD_DIM = 128
CHUNK = 128
LRU_C = 8.0
NORM_EPS = 1e-6
TOP_K = 2
VMEM_LIMIT = 56 << 20


def _cp(*sem, vmem=VMEM_LIMIT):
    return pltpu.CompilerParams(dimension_semantics=sem, vmem_limit_bytes=vmem)


def _tile(n, pref, mult=8):
    if n <= pref:
        return n
    t = (pref // mult) * mult
    while t >= mult:
        if n % t == 0:
            return t
        t -= mult
    return n


def _gelu(x):
    return x * (0.5 * (1.0 + jnp.tanh(0.7978845608028654 * (x + 0.044715 * (x * x * x)))))


def _sigmoid(x):
    return 1.0 / (1.0 + jnp.exp(-x))


def _silu(x):
    return x * _sigmoid(x)


def _log_sigmoid(x):
    return jnp.minimum(x, 0.0) - jnp.log1p(jnp.exp(-jnp.abs(x)))


def _neg_expm1(y):
    return -jnp.tanh(0.5 * y) * (jnp.exp(y) + 1.0)


def _rmsnorm(x, g):
    return x * lax.rsqrt(jnp.mean(x * x, axis=-1, keepdims=True) + NORM_EPS) * g


def _layernorm(x, g, b):
    xc = x - jnp.mean(x, axis=-1, keepdims=True)
    var = jnp.mean(xc * xc, axis=-1, keepdims=True)
    return xc * lax.rsqrt(var + NORM_EPS) * g + b


def _dot(a, b):
    return jnp.dot(a, b, preferred_element_type=F32)


def _dot_t(a, b):
    return lax.dot_general(a, b, (((1,), (1,)), ((), ())), preferred_element_type=F32)


def _mm(a, w, exact):
    if exact:
        return jnp.dot(a.astype(F32), w, preferred_element_type=F32, precision=lax.Precision.HIGHEST)
    return _dot(a.astype(BF16), w.astype(BF16))


def _nmm_kernel(x_ref, g_ref, w_ref, o_ref, h_ref, *, norm, exact):
    @pl.when(pl.program_id(1) == 0)
    def _():
        x = x_ref[...]
        if norm:
            x = _rmsnorm(x, g_ref[...])
        h_ref[...] = x.astype(h_ref.dtype)

    o_ref[...] = _mm(h_ref[...], w_ref[...], exact)


def norm_matmul(x, g, w, *, tm=1024, tn=512, norm=True, exact=False):
    m, k = x.shape
    n = w.shape[1]
    tm, tn = _tile(m, tm), _tile(n, tn, LANES)
    return pl.pallas_call(
        functools.partial(_nmm_kernel, norm=norm, exact=exact),
        grid=(m // tm, n // tn),
        in_specs=[pl.BlockSpec((tm, k), lambda i, j: (i, 0)),
                  pl.BlockSpec((1, k), lambda i, j: (0, 0)),
                  pl.BlockSpec((k, tn), lambda i, j: (0, j))],
        out_specs=pl.BlockSpec((tm, tn), lambda i, j: (i, j)),
        out_shape=jax.ShapeDtypeStruct((m, n), F32),
        scratch_shapes=[pltpu.VMEM((tm, k), F32 if exact else BF16)],
        compiler_params=_cp("parallel", "arbitrary"),
        name="norm_matmul",
    )(x, g.reshape(1, k), w)


def _mmres_kernel(*refs, n_in, exact):
    a_refs, w_refs, r_ref, o_ref = refs[:n_in], refs[n_in:2 * n_in], refs[2 * n_in], refs[2 * n_in + 1]
    acc = r_ref[...]
    for a_ref, w_ref in zip(a_refs, w_refs):
        acc = acc + _mm(a_ref[...], w_ref[...], exact)
    o_ref[...] = acc


def matmul_residual(parts, w, res, *, tm=1024, tn=512, exact=False):
    m, n = res.shape
    tm, tn = _tile(m, tm), _tile(n, tn, LANES)
    kp = parts[0].shape[1]
    assert all(p.shape[1] == kp for p in parts) and kp * len(parts) == w.shape[0]
    n_in = len(parts)
    in_specs = [pl.BlockSpec((tm, kp), lambda i, j: (i, 0)) for _ in parts]
    in_specs += [pl.BlockSpec((kp, tn), functools.partial(lambda i, j, c: (c, j), c=c)) for c in range(n_in)]
    in_specs += [pl.BlockSpec((tm, tn), lambda i, j: (i, j))]
    return pl.pallas_call(
        functools.partial(_mmres_kernel, n_in=n_in, exact=exact),
        grid=(m // tm, n // tn),
        in_specs=in_specs,
        out_specs=pl.BlockSpec((tm, tn), lambda i, j: (i, j)),
        out_shape=jax.ShapeDtypeStruct((m, n), F32),
        compiler_params=_cp("parallel", "arbitrary"),
        name="matmul_residual",
    )(*parts, *([w] * n_in), res)


def _lru_prompt_kernel(gate_ref, xr_ref, wc_ref, bc_ref, wra_ref, bra_ref, wri_ref, bri_ref, lam_ref,
                       ya_ref, hl_ref, xl_ref, xp_ref, *, t, width):
    pad = 8
    xr = xr_ref[...]
    xp_ref[0:pad, :] = jnp.zeros((pad, LANES), F32)
    xp_ref[pad:pad + t, :] = xr
    xc = bc_ref[0]
    for j in range(width):
        xc = xc + wc_ref[0, j:j + 1, :] * xp_ref[pl.ds(pad - (width - 1) + j, t), :]
    xcb = xc.astype(BF16)
    r = _sigmoid(_dot(xcb, wra_ref[0].astype(BF16)) + bra_ref[0])
    ig = _sigmoid(_dot(xcb, wri_ref[0].astype(BF16)) + bri_ref[0])
    log_a = LRU_C * r * _log_sigmoid(lam_ref[0])
    a = jnp.exp(log_a)
    b = jnp.sqrt(_neg_expm1(2.0 * log_a)) * (ig * xc)
    row = lax.broadcasted_iota(jnp.int32, (t, LANES), 0)
    s = 1
    while s < t:
        keep = row >= s
        a_sh = jnp.where(keep, pltpu.roll(a, s, 0), 1.0)
        b_sh = jnp.where(keep, pltpu.roll(b, s, 0), 0.0)
        b = b + a * b_sh
        a = a * a_sh
        s *= 2
    ya_ref[...] = (_gelu(gate_ref[...]) * b).astype(ya_ref.dtype)
    hl_ref[0] = b[t - 8:t, :]
    xl_ref[0] = xr[t - 8:t, :]


def lru_prompt(z, n_seq, t, wc, bc, wra, bra, wri, bri, lam):
    a_width = wc.shape[1]
    nh = a_width // LANES
    width = wc.shape[0]
    wc_h = wc.reshape(width, nh, LANES).transpose(1, 0, 2)
    vec = lambda v: v.reshape(nh, 1, LANES)
    vspec = pl.BlockSpec((1, 1, LANES), lambda n, h: (h, 0, 0))
    mspec = pl.BlockSpec((1, LANES, LANES), lambda n, h: (h, 0, 0))
    return pl.pallas_call(
        functools.partial(_lru_prompt_kernel, t=t, width=width),
        grid=(n_seq, nh),
        in_specs=[pl.BlockSpec((t, LANES), lambda n, h: (n, h)),
                  pl.BlockSpec((t, LANES), lambda n, h, nh=nh: (n, nh + h)),
                  pl.BlockSpec((1, width, LANES), lambda n, h: (h, 0, 0)), vspec,
                  mspec, vspec, mspec, vspec, vspec],
        out_specs=[pl.BlockSpec((t, LANES), lambda n, h: (n, h)),
                   pl.BlockSpec((1, 8, LANES), lambda n, h: (n, 0, h)),
                   pl.BlockSpec((1, 8, LANES), lambda n, h: (n, 0, h))],
        out_shape=[jax.ShapeDtypeStruct((n_seq * t, a_width), BF16),
                   jax.ShapeDtypeStruct((n_seq, 8, a_width), F32),
                   jax.ShapeDtypeStruct((n_seq, 8, a_width), F32)],
        scratch_shapes=[pltpu.VMEM((t + 8, LANES), F32)],
        compiler_params=_cp("parallel", "parallel"),
        name="lru_prompt",
    )(z, z, wc_h, vec(bc), wra, vec(bra), wri, vec(bri), vec(lam))


def _sgu_prompt_kernel(u_ref, v_ref, lng_ref, lnb_ref, ws_ref, bst_ref, yb_ref, *, tt, ng):
    vn = _layernorm(_gelu(v_ref[...]), lng_ref[...], lnb_ref[...])
    ug = _gelu(u_ref[...])
    r = lax.broadcasted_iota(jnp.int32, (CHUNK, CHUNK), 0)
    c = lax.broadcasted_iota(jnp.int32, (CHUNK, CHUNK), 1)
    causal = c <= r
    bst = bst_ref[...]
    for g in range(ng):
        wg = jnp.where(causal, ws_ref[g], 0.0).astype(BF16)
        bias = bst[:, g:g + 1]
        for ch in range(tt // CHUNK):
            rows = slice(ch * CHUNK, (ch + 1) * CHUNK)
            cols = slice(g * LANES, (g + 1) * LANES)
            s = _dot(wg, vn[rows, cols].astype(BF16)) + bias
            yb_ref[rows, cols] = (ug[rows, cols] * s).astype(yb_ref.dtype)


def sgu_prompt(z, n_seq, t, a_width, lng, lnb, ws, bs, *, tt=512):
    m = n_seq * t
    b_width = lng.shape[0]
    ng = b_width // LANES
    tt = _tile(t, tt, CHUNK)
    ucol, vcol = (2 * a_width) // b_width, (2 * a_width) // b_width + 1
    return pl.pallas_call(
        functools.partial(_sgu_prompt_kernel, tt=tt, ng=ng),
        grid=(m // tt,),
        in_specs=[pl.BlockSpec((tt, b_width), lambda i, c=ucol: (i, c)),
                  pl.BlockSpec((tt, b_width), lambda i, c=vcol: (i, c)),
                  pl.BlockSpec((1, b_width), lambda i: (0, 0)),
                  pl.BlockSpec((1, b_width), lambda i: (0, 0)),
                  pl.BlockSpec((ng, CHUNK, CHUNK), lambda i: (0, 0, 0)),
                  pl.BlockSpec((CHUNK, ng), lambda i: (0, 0))],
        out_specs=pl.BlockSpec((tt, b_width), lambda i: (i, 0)),
        out_shape=jax.ShapeDtypeStruct((m, b_width), BF16),
        compiler_params=_cp("parallel"),
        name="sgu_prompt",
    )(z, z, lng.reshape(1, -1), lnb.reshape(1, -1), ws, bs.T)


def _mixer0_sample_kernel(z_ref, h0_ref, cv_ref, wc_ref, bc_ref, wra_ref, bra_ref, wri_ref, bri_ref, lam_ref,
                          lng_ref, lnb_ref, ws0_ref, bs0_ref, yab_ref, h_ref, v_ref, *, a_width, width):
    aw = a_width
    gate, xr, u, v = (z_ref[:, k * aw:(k + 1) * aw] for k in range(4))
    xc = bc_ref[...] + wc_ref[width - 1:width, :] * xr
    for j in range(width - 1):
        xc = xc + wc_ref[j:j + 1, :] * cv_ref[:, j * aw:(j + 1) * aw]
    nh = aw // LANES
    rs, gs = [], []
    for h in range(nh):
        xh = xc[:, h * LANES:(h + 1) * LANES]
        rs.append(_mm(xh, wra_ref[h], True))
        gs.append(_mm(xh, wri_ref[h], True))
    r = _sigmoid(jnp.concatenate(rs, axis=1) + bra_ref[...])
    ig = _sigmoid(jnp.concatenate(gs, axis=1) + bri_ref[...])
    log_a = LRU_C * r * _log_sigmoid(lam_ref[...])
    hs = jnp.exp(log_a) * h0_ref[...] + jnp.sqrt(_neg_expm1(2.0 * log_a)) * (ig * xc)
    h_ref[...] = hs
    vn = _layernorm(_gelu(v), lng_ref[...], lnb_ref[...])
    v_ref[...] = vn
    yab_ref[:, 0:aw] = _gelu(gate) * hs
    yab_ref[:, aw:2 * aw] = _gelu(u) * (ws0_ref[...] * vn + bs0_ref[...])


def mixer0_sample(z, h0, conv0, wc, bc, wra, bra, wri, bri, lam, lng, lnb, ws, bs):
    n, a_width = h0.shape
    width = wc.shape[0]
    row = lambda v: v.reshape(1, -1)
    ws0 = jnp.repeat(ws[:, 0, 0], LANES).reshape(1, -1)
    bs0 = jnp.repeat(bs[:, 0], LANES).reshape(1, -1)
    args = (z, h0, conv0.reshape(n, -1), wc, row(bc), wra, row(bra), wri, row(bri), row(lam),
            row(lng), row(lnb), ws0, bs0)
    return pl.pallas_call(
        functools.partial(_mixer0_sample_kernel, a_width=a_width, width=width),
        out_shape=[jax.ShapeDtypeStruct((n, 2 * a_width), F32),
                   jax.ShapeDtypeStruct((n, a_width), F32),
                   jax.ShapeDtypeStruct((n, a_width), F32)],
        compiler_params=pltpu.CompilerParams(vmem_limit_bytes=VMEM_LIMIT),
        name="mixer0_sample",
    )(*args)


def _logf_kernel(fl_ref, b_ref, lf_ref, c_ref, *, t):
    lf = _log_sigmoid(fl_ref[0] + b_ref[...])
    lf_ref[0] = lf
    lane = lax.broadcasted_iota(jnp.int32, lf.shape, 1)
    c = lf
    s = 1
    while s < t:
        c = c + jnp.where(lane >= s, pltpu.roll(c, s, 1), 0.0)
        s *= 2
    c_ref[0] = c


def logf_cumsum(fl_t, b_forget):
    n, h, t = fl_t.shape
    spec = pl.BlockSpec((1, h, t), lambda i: (i, 0, 0))
    return pl.pallas_call(
        functools.partial(_logf_kernel, t=t),
        grid=(n,),
        in_specs=[spec, pl.BlockSpec((h, 1), lambda i: (0, 0))],
        out_specs=[spec, spec],
        out_shape=[jax.ShapeDtypeStruct((n, h, t), F32)] * 2,
        compiler_params=_cp("parallel"),
        name="logf_cumsum",
    )(fl_t, b_forget.reshape(h, 1))


def _fox_prompt_kernel(q_ref, k_ref, v_ref, ck_ref, o_ref, m_ref, l_ref, acc_ref, *, scale, tq, tk):
    qi, ki = pl.program_id(2), pl.program_id(3)

    @pl.when(ki == 0)
    def _():
        m_ref[...] = jnp.full(m_ref.shape, -jnp.inf, F32)
        l_ref[...] = jnp.zeros(l_ref.shape, F32)
        acc_ref[...] = jnp.zeros(acc_ref.shape, F32)

    @pl.when(ki <= qi)
    def _():
        s = _dot_t(q_ref[...].astype(BF16), k_ref[...].astype(BF16)) * scale - ck_ref[0]
        rows = qi * tq + lax.broadcasted_iota(jnp.int32, (tq, tk), 0)
        cols = ki * tk + lax.broadcasted_iota(jnp.int32, (tq, tk), 1)
        s = jnp.where(cols <= rows, s, -jnp.inf)
        m_prev = m_ref[...]
        m_new = jnp.maximum(m_prev, jnp.max(s, axis=1, keepdims=True))
        alpha = jnp.exp(m_prev - m_new)
        p = jnp.exp(s - m_new)
        l_ref[...] = alpha * l_ref[...] + jnp.sum(p, axis=1, keepdims=True)
        acc_ref[...] = alpha * acc_ref[...] + _dot(p.astype(BF16), v_ref[...].astype(BF16))
        m_ref[...] = m_new

    @pl.when(ki == qi)
    def _():
        o_ref[...] = (acc_ref[...] / l_ref[...]).astype(o_ref.dtype)


def fox_prompt(qkv, c_t, n_seq, t, nh, *, tq=512):
    tq = _tile(t, tq, LANES)
    nq = t // tq
    return pl.pallas_call(
        functools.partial(_fox_prompt_kernel, scale=HEAD_DIM ** -0.5, tq=tq, tk=tq),
        grid=(n_seq, nh, nq, nq),
        in_specs=[pl.BlockSpec((tq, HEAD_DIM), lambda n, h, qi, ki: (n * nq + qi, h)),
                  pl.BlockSpec((tq, HEAD_DIM), lambda n, h, qi, ki: (n * nq + jnp.minimum(ki, qi), nh + h)),
                  pl.BlockSpec((tq, HEAD_DIM), lambda n, h, qi, ki: (n * nq + jnp.minimum(ki, qi), 2 * nh + h)),
                  pl.BlockSpec((1, 1, tq), lambda n, h, qi, ki: (n * nh + h, 0, jnp.minimum(ki, qi)))],
        out_specs=pl.BlockSpec((tq, HEAD_DIM), lambda n, h, qi, ki: (n * nq + qi, h)),
        out_shape=jax.ShapeDtypeStruct((n_seq * t, nh * HEAD_DIM), BF16),
        scratch_shapes=[pltpu.VMEM((tq, 1), F32), pltpu.VMEM((tq, 1), F32), pltpu.VMEM((tq, HEAD_DIM), F32)],
        compiler_params=_cp("parallel", "parallel", "parallel", "arbitrary"),
        name="fox_prompt",
    )(qkv, qkv, qkv, c_t)


def _fox_sample_kernel(pt_ref, q_ref, kn_ref, vn_ref, lfn_ref, *rest, scale, nh, n_steps, g_pages, page):
    kc = rest[0:g_pages]
    vc = rest[g_pages:2 * g_pages]
    lfc = rest[2 * g_pages:3 * g_pages]
    o_ref, m_ref, l_ref, acc_ref, r_ref = rest[3 * g_pages:]
    step = pl.program_id(1)
    q8 = q_ref[0]

    @pl.when(step == 0)
    def _():
        s_new = jnp.sum(q8 * kn_ref[0], axis=1, keepdims=True) * scale
        m_ref[...] = jnp.broadcast_to(s_new, m_ref.shape)
        l_ref[...] = jnp.ones(l_ref.shape, F32)
        acc_ref[...] = vn_ref[0]
        r_ref[...] = lfn_ref[0]

    m = m_ref[:, 0:1]
    l = l_ref[:, 0:1]
    acc = acc_ref[...]
    run = r_ref[...]
    sel = (lax.broadcasted_iota(jnp.int32, (page, nh, LANES), 0)
           == lax.broadcasted_iota(jnp.int32, (page, nh, LANES), 2))
    lane = lax.broadcasted_iota(jnp.int32, (nh, page), 1)
    for g in range(g_pages):
        s3 = jnp.sum(kc[g][0] * q8[None], axis=-1, keepdims=True)
        s_hp = jnp.sum(jnp.where(sel, s3, 0.0), axis=0)
        lf = lfc[g][0]
        suf = lf
        sh = 1
        while sh < page:
            suf = suf + jnp.where(lane + sh < page, pltpu.roll(suf, page - sh, 1), 0.0)
            sh *= 2
        logits = s_hp * scale + (run + suf - lf)
        run = run + suf[:, 0:1]
        m_new = jnp.maximum(m, jnp.max(logits, axis=1, keepdims=True))
        alpha = jnp.exp(m - m_new)
        p = jnp.exp(logits - m_new)
        l = alpha * l + jnp.sum(p, axis=1, keepdims=True)
        p3 = jnp.sum(jnp.where(sel, p[None], 0.0), axis=-1, keepdims=True)
        acc = alpha * acc + jnp.sum(p3 * vc[g][0], axis=0)
        m = m_new
    m_ref[...] = jnp.broadcast_to(m, m_ref.shape)
    l_ref[...] = jnp.broadcast_to(l, l_ref.shape)
    acc_ref[...] = acc
    r_ref[...] = run

    @pl.when(step == n_steps - 1)
    def _():
        o_ref[0] = acc / l


def fox_sample(page_table, q, k_new, v_new, lf_new, cache_k, cache_v, cache_lf_t, *, g_pages=8):
    n, nh, _ = q.shape
    n_pages = page_table.shape[1]
    page = cache_k.shape[1]
    g_pages = _tile(n_pages, g_pages, 1)
    n_steps = n_pages // g_pages
    lf_b = jnp.broadcast_to(lf_new[:, :, None], (n, nh, LANES))
    rspec = pl.BlockSpec((1, nh, LANES), lambda i, s, pt: (i, 0, 0))

    def pidx(g):
        return lambda i, s, pt: (pt[i * n_pages + (n_pages - 1 - (s * g_pages + g))], 0, 0, 0)

    def lidx(g):
        return lambda i, s, pt: (pt[i * n_pages + (n_pages - 1 - (s * g_pages + g))], 0, 0)

    kspecs = [pl.BlockSpec((1, page, nh, LANES), pidx(g)) for g in range(g_pages)]
    lspecs = [pl.BlockSpec((1, nh, page), lidx(g)) for g in range(g_pages)]
    grid_spec = pltpu.PrefetchScalarGridSpec(
        num_scalar_prefetch=1,
        grid=(n, n_steps),
        in_specs=[rspec, rspec, rspec, rspec] + kspecs + kspecs + lspecs,
        out_specs=pl.BlockSpec((1, nh, LANES), lambda i, s, pt: (i, 0, 0)),
        scratch_shapes=[pltpu.VMEM((nh, LANES), F32), pltpu.VMEM((nh, LANES), F32),
                        pltpu.VMEM((nh, LANES), F32), pltpu.VMEM((nh, LANES), F32)],
    )
    return pl.pallas_call(
        functools.partial(_fox_sample_kernel, scale=HEAD_DIM ** -0.5, nh=nh, n_steps=n_steps,
                          g_pages=g_pages, page=page),
        grid_spec=grid_spec,
        out_shape=jax.ShapeDtypeStruct((n, nh, LANES), F32),
        compiler_params=_cp("parallel", "arbitrary"),
        name="fox_sample",
    )(page_table.reshape(-1), q, k_new, v_new, lf_b,
      *([cache_k] * g_pages), *([cache_v] * g_pages), *([cache_lf_t] * g_pages))


def _convd_prompt_kernel(ga_ref, gb_ref, wc_ref, bc_ref, lng_ref, lnb_ref, yd_ref, tail_ref, xp_ref, *, tt, width):
    pad = 32

    @pl.when(pl.program_id(1) == 0)
    def _():
        xp_ref[0:pad, :] = jnp.zeros((pad, xp_ref.shape[1]), F32)

    glu = ga_ref[...] * _sigmoid(gb_ref[...])
    xp_ref[pad:pad + tt, :] = glu
    dc = bc_ref[...]
    for j in range(width):
        dc = dc + wc_ref[j:j + 1, :] * xp_ref[pl.ds(pad - (width - 1) + j, tt), :]
    yd_ref[...] = _silu(_layernorm(dc, lng_ref[...], lnb_ref[...])).astype(yd_ref.dtype)
    tail_ref[0] = glu[tt - pad:tt, :]
    xp_ref[0:pad, :] = glu[tt - pad:tt, :]


def convd_prompt(gd, n_seq, t, wc, bc, lng, lnb, *, tt=512):
    width, d_width = wc.shape
    assert width - 1 <= 32
    tt = _tile(t, tt, 32)
    nt = t // tt
    row = lambda v: v.reshape(1, -1)
    cspec = pl.BlockSpec((1, d_width), lambda n, i: (0, 0))
    return pl.pallas_call(
        functools.partial(_convd_prompt_kernel, tt=tt, width=width),
        grid=(n_seq, nt),
        in_specs=[pl.BlockSpec((tt, d_width), lambda n, i: (n * nt + i, 0)),
                  pl.BlockSpec((tt, d_width), lambda n, i: (n * nt + i, 1)),
                  pl.BlockSpec((width, d_width), lambda n, i: (0, 0)), cspec, cspec, cspec],
        out_specs=[pl.BlockSpec((tt, d_width), lambda n, i: (n * nt + i, 0)),
                   pl.BlockSpec((1, 32, d_width), lambda n, i: (n, 0, 0))],
        out_shape=[jax.ShapeDtypeStruct((n_seq * t, d_width), BF16),
                   jax.ShapeDtypeStruct((n_seq, 32, d_width), F32)],
        scratch_shapes=[pltpu.VMEM((tt + 32, d_width), F32)],
        compiler_params=_cp("parallel", "arbitrary"),
        name="convd_prompt",
    )(gd, gd, wc, row(bc), row(lng), row(lnb))


def _convd_sample_kernel(gd_ref, st_ref, wc_ref, bc_ref, lng_ref, lnb_ref, yd_ref, glu_ref, *, width, d_width):
    glu = gd_ref[:, 0:d_width] * _sigmoid(gd_ref[:, d_width:2 * d_width])
    glu_ref[...] = glu
    dc = bc_ref[...] + wc_ref[width - 1:width, :] * glu
    for j in range(width - 1):
        dc = dc + wc_ref[j:j + 1, :] * st_ref[:, j, :]
    yd_ref[...] = _silu(_layernorm(dc, lng_ref[...], lnb_ref[...]))


def convd_sample(gd, state, wc, bc, lng, lnb, *, tb=16):
    n = gd.shape[0]
    width, d_width = wc.shape
    tb = _tile(n, tb)
    row = lambda v: v.reshape(1, -1)
    cspec = pl.BlockSpec((1, d_width), lambda i: (0, 0))
    return pl.pallas_call(
        functools.partial(_convd_sample_kernel, width=width, d_width=d_width),
        grid=(n // tb,),
        in_specs=[pl.BlockSpec((tb, 2 * d_width), lambda i: (i, 0)),
                  pl.BlockSpec((tb, width - 1, d_width), lambda i: (i, 0, 0)),
                  pl.BlockSpec((width, d_width), lambda i: (0, 0)), cspec, cspec, cspec],
        out_specs=[pl.BlockSpec((tb, d_width), lambda i: (i, 0)),
                   pl.BlockSpec((tb, d_width), lambda i: (i, 0))],
        out_shape=[jax.ShapeDtypeStruct((n, d_width), F32)] * 2,
        compiler_params=_cp("parallel"),
        name="convd_sample",
    )(gd, state, wc, row(bc), row(lng), row(lnb))


def _xattn_prompt_kernel(q_ref, k_ref, v_ref, o_ref, *, nh, scale):
    for h in range(nh):
        cols = slice(h * HEAD_DIM, (h + 1) * HEAD_DIM)
        s = _dot_t(q_ref[:, cols].astype(BF16), k_ref[0, :, cols].astype(BF16)) * scale
        e = jnp.exp(s - jnp.max(s, axis=1, keepdims=True))
        p = e / jnp.sum(e, axis=1, keepdims=True)
        o_ref[:, cols] = _dot(p.astype(BF16), v_ref[0, :, cols].astype(BF16)).astype(o_ref.dtype)


def xattn_prompt(q, mk, mv, n_seq, t, *, tt=512):
    xw = q.shape[1]
    n_mem = mk.shape[1]
    tt = _tile(t, tt, 16)
    nt = t // tt
    kspec = pl.BlockSpec((1, n_mem, xw), lambda n, i: (n, 0, 0))
    return pl.pallas_call(
        functools.partial(_xattn_prompt_kernel, nh=xw // HEAD_DIM, scale=HEAD_DIM ** -0.5),
        grid=(n_seq, nt),
        in_specs=[pl.BlockSpec((tt, xw), lambda n, i: (n * nt + i, 0)), kspec, kspec],
        out_specs=pl.BlockSpec((tt, xw), lambda n, i: (n * nt + i, 0)),
        out_shape=jax.ShapeDtypeStruct((n_seq * t, xw), BF16),
        compiler_params=_cp("parallel", "parallel"),
        name="xattn_prompt",
    )(q, mk, mv)


def _xattn_sample_kernel(q_ref, k_ref, v_ref, o_ref, *, tb, scale):
    for b in range(tb):
        s3 = jnp.sum(k_ref[0, b] * q_ref[b][None], axis=-1, keepdims=True) * scale
        e = jnp.exp(s3 - jnp.max(s3, axis=0, keepdims=True))
        p = e / jnp.sum(e, axis=0, keepdims=True)
        o_ref[b] = jnp.sum(p * v_ref[0, b], axis=0)


def xattn_sample(q, mem_k, mem_v, layer, *, tb=4):
    n, nh, _ = q.shape
    n_mem = mem_k.shape[2]
    tb = _tile(n, tb, 1)
    kspec = pl.BlockSpec((1, tb, n_mem, nh, HEAD_DIM), lambda i: (layer, i, 0, 0, 0))
    qspec = pl.BlockSpec((tb, nh, HEAD_DIM), lambda i: (i, 0, 0))
    return pl.pallas_call(
        functools.partial(_xattn_sample_kernel, tb=tb, scale=HEAD_DIM ** -0.5),
        grid=(n // tb,),
        in_specs=[qspec, kspec, kspec],
        out_specs=qspec,
        out_shape=jax.ShapeDtypeStruct((n, nh, HEAD_DIM), F32),
        compiler_params=_cp("parallel"),
        name="xattn_sample",
    )(q, mem_k, mem_v)


def _ffn_kernel(x_ref, g_ref, wg_ref, wu_ref, wd_ref, o_ref, h_ref, *, exact):
    j = pl.program_id(1)

    @pl.when(j == 0)
    def _():
        x = x_ref[...]
        h_ref[...] = _rmsnorm(x, g_ref[...]).astype(h_ref.dtype)
        o_ref[...] = x

    h = h_ref[...]
    a = _silu(_mm(h, wg_ref[...], exact)) * _mm(h, wu_ref[...], exact)
    o_ref[...] += _mm(a, wd_ref[...], exact)


def ffn_swiglu(x, g, wg, wu, wd, *, tm=1024, tf=256, exact=False):
    m, d = x.shape
    f = wg.shape[1]
    tm, tf = _tile(m, tm), _tile(f, tf, LANES)
    return pl.pallas_call(
        functools.partial(_ffn_kernel, exact=exact),
        grid=(m // tm, f // tf),
        in_specs=[pl.BlockSpec((tm, d), lambda i, j: (i, 0)),
                  pl.BlockSpec((1, d), lambda i, j: (0, 0)),
                  pl.BlockSpec((d, tf), lambda i, j: (0, j)),
                  pl.BlockSpec((d, tf), lambda i, j: (0, j)),
                  pl.BlockSpec((tf, d), lambda i, j: (j, 0))],
        out_specs=pl.BlockSpec((tm, d), lambda i, j: (i, 0)),
        out_shape=jax.ShapeDtypeStruct((m, d), F32),
        scratch_shapes=[pltpu.VMEM((tm, d), F32 if exact else BF16)],
        compiler_params=_cp("parallel", "arbitrary"),
        name="ffn_swiglu",
    )(x, g.reshape(1, d), wg, wu, wd)


def _route_kernel(lg_ref, rt_ref, *, n_exp):
    lg = lg_ref[...]
    lane = lax.broadcasted_iota(jnp.int32, lg.shape, 1)
    neg = jnp.float32(-jnp.inf)
    big = jnp.int32(lg.shape[1])
    lg = jnp.where(lane < n_exp, lg, neg)
    m1 = jnp.max(lg, axis=1, keepdims=True)
    i1 = jnp.min(jnp.where(lg == m1, lane, big), axis=1, keepdims=True)
    rest = jnp.where(lane == i1, neg, lg)
    m2 = jnp.max(rest, axis=1, keepdims=True)
    i2 = jnp.min(jnp.where(rest == m2, lane, big), axis=1, keepdims=True)
    e2 = jnp.exp(m2 - m1)
    g1 = 1.0 / (1.0 + e2)
    g2 = e2 / (1.0 + e2)
    rt_ref[...] = (jnp.where(lane == 0, i1.astype(F32), 0.0) + jnp.where(lane == 1, i2.astype(F32), 0.0)
                   + jnp.where(lane == 2, g1, 0.0) + jnp.where(lane == 3, g2, 0.0))


def route_top2(logits, n_exp, *, tm=512):
    m, w = logits.shape
    tm = _tile(m, tm)
    spec = pl.BlockSpec((tm, w), lambda i: (i, 0))
    return pl.pallas_call(
        functools.partial(_route_kernel, n_exp=n_exp),
        grid=(m // tm,), in_specs=[spec], out_specs=spec,
        out_shape=jax.ShapeDtypeStruct((m, w), F32),
        compiler_params=_cp("parallel"),
        name="route_top2",
    )(logits)


def _routing_tables(route, n_exp, tm):
    m = route.shape[0]
    e = jnp.concatenate([route[:, 0], route[:, 1]]).astype(jnp.int32)
    onehot = (e[:, None] == jnp.arange(n_exp, dtype=jnp.int32)[None, :]).astype(jnp.int32)
    csum = jnp.cumsum(onehot, axis=0)
    rank = jnp.take_along_axis(csum, e[:, None], axis=1)[:, 0] - 1
    counts = csum[-1]
    tiles_per = (counts + tm - 1) // tm
    tile_end = jnp.cumsum(tiles_per)
    n_used = tile_end[-1]
    pos = (tile_end - tiles_per)[e] * tm + rank
    n_tiles = (TOP_K * m + tm - 1) // tm + n_exp
    tok = jnp.concatenate([jnp.arange(m, dtype=jnp.int32)] * TOP_K)
    src = jnp.zeros((n_tiles * tm,), jnp.int32).at[pos].set(tok)
    tile_ids = jnp.minimum(jnp.arange(n_tiles, dtype=jnp.int32), n_used - 1)
    tile_exp = jnp.minimum(jnp.searchsorted(tile_end, tile_ids, side='right'), n_exp - 1).astype(jnp.int32)
    return src, tile_exp, n_used.reshape(1).astype(jnp.int32), pos.astype(jnp.int32), n_tiles


def _moe_expert_kernel(src_ref, texp_ref, nused_ref, x_hbm, g_ref, wg_ref, wu_ref, wd_ref, o_ref,
                       xbuf, h_ref, sem, *, tm):
    t, j = pl.program_id(0), pl.program_id(1)
    n_used = nused_ref[0]
    slot = t % 2

    def row_copy(tile, slot_, i):
        r = src_ref[tile * tm + i]
        return pltpu.make_async_copy(x_hbm.at[pl.ds(r, 1)], xbuf.at[slot_, pl.ds(i, 1)], sem.at[slot_])

    def start_tile(tile, slot_):
        def body(i, c):
            row_copy(tile, slot_, i).start()
            return c
        lax.fori_loop(0, tm, body, 0, unroll=8)

    def wait_tile(tile, slot_):
        def body(i, c):
            row_copy(tile, slot_, i).wait()
            return c
        lax.fori_loop(0, tm, body, 0, unroll=8)

    @pl.when((j == 0) & (t < n_used))
    def _():
        @pl.when(t == 0)
        def _():
            start_tile(0, 0)

        @pl.when(t + 1 < n_used)
        def _():
            start_tile(t + 1, 1 - slot)

        wait_tile(t, slot)
        h_ref[...] = _rmsnorm(xbuf[slot], g_ref[...]).astype(BF16)

    @pl.when(t < n_used)
    def _():
        h = h_ref[...]
        a = _silu(_dot(h, wg_ref[0].astype(BF16))) * _dot(h, wu_ref[0].astype(BF16))
        y = _dot(a.astype(BF16), wd_ref[0].astype(BF16))

        @pl.when(j == 0)
        def _():
            o_ref[...] = y

        @pl.when(j > 0)
        def _():
            o_ref[...] += y

    @pl.when((t >= n_used) & (j == 0))
    def _():
        o_ref[...] = jnp.zeros(o_ref.shape, F32)


def moe_experts(x, g, src, tile_exp, n_used, n_tiles, wg, wu, wd, *, tm, tf=512):
    m, d = x.shape
    f = wg.shape[2]
    tf = _tile(f, tf, LANES)
    nj = f // tf

    def widx(t, j, src_r, texp_r, nused_r):
        return (texp_r[t], 0, jnp.where(t < nused_r[0], j, nj - 1))

    def didx(t, j, src_r, texp_r, nused_r):
        return (texp_r[t], jnp.where(t < nused_r[0], j, nj - 1), 0)

    grid_spec = pltpu.PrefetchScalarGridSpec(
        num_scalar_prefetch=3,
        grid=(n_tiles, nj),
        in_specs=[pl.BlockSpec(memory_space=pl.ANY),
                  pl.BlockSpec((1, d), lambda t, j, *_: (0, 0)),
                  pl.BlockSpec((1, d, tf), widx),
                  pl.BlockSpec((1, d, tf), widx),
                  pl.BlockSpec((1, tf, d), didx)],
        out_specs=pl.BlockSpec((tm, d), lambda t, j, *_: (t, 0)),
        scratch_shapes=[pltpu.VMEM((2, tm, d), F32), pltpu.VMEM((tm, d), BF16), pltpu.SemaphoreType.DMA((2,))],
    )
    return pl.pallas_call(
        functools.partial(_moe_expert_kernel, tm=tm),
        grid_spec=grid_spec,
        out_shape=jax.ShapeDtypeStruct((n_tiles * tm, d), F32),
        compiler_params=_cp("arbitrary", "arbitrary"),
        name="moe_experts",
    )(src, tile_exp, n_used, x, g.reshape(1, d), wg, wu, wd)


def _moe_combine_kernel(pos_ref, xp_ref, xs_ref, rt_ref, gf_ref, ys_hbm, op_ref, os_ref, ybuf, sem,
                        *, tc, m, n_p_tiles):
    i = pl.program_id(0)
    n = pl.num_programs(0)
    slot = i % 2

    def row_copy(tile, slot_, k, r):
        p = pos_ref[k * m + tile * tc + r]
        return pltpu.make_async_copy(ys_hbm.at[pl.ds(p, 1)], ybuf.at[slot_, k, pl.ds(r, 1)], sem.at[slot_])

    def start_tile(tile, slot_):
        def body(r, c):
            for k in range(TOP_K):
                row_copy(tile, slot_, k, r).start()
            return c
        lax.fori_loop(0, tc, body, 0, unroll=4)

    def wait_tile(tile, slot_):
        def body(r, c):
            for k in range(TOP_K):
                row_copy(tile, slot_, k, r).wait()
            return c
        lax.fori_loop(0, tc, body, 0, unroll=4)

    @pl.when(i == 0)
    def _():
        start_tile(0, 0)

    @pl.when(i + 1 < n)
    def _():
        start_tile(i + 1, 1 - slot)

    wait_tile(i, slot)
    rt = rt_ref[...]
    mix = rt[:, 2:3] * ybuf[slot, 0] + rt[:, 3:4] * ybuf[slot, 1]

    @pl.when(i < n_p_tiles)
    def _():
        op_ref[...] = _rmsnorm(xp_ref[...] + mix, gf_ref[...])

    @pl.when(i >= n_p_tiles)
    def _():
        os_ref[...] = _rmsnorm(xs_ref[...] + mix, gf_ref[...])


def moe_combine(x_p, x_s, route, pos, ys, g_final, *, tc=128):
    mp, d = x_p.shape
    ms = x_s.shape[0]
    tc = _tile(ms, tc)
    assert mp % tc == 0
    n_p_tiles, n_s_tiles = mp // tc, ms // tc
    m = mp + ms
    grid_spec = pltpu.PrefetchScalarGridSpec(
        num_scalar_prefetch=1,
        grid=(n_p_tiles + n_s_tiles,),
        in_specs=[pl.BlockSpec((tc, d), lambda i, *_: (jnp.minimum(i, n_p_tiles - 1), 0)),
                  pl.BlockSpec((tc, d), lambda i, *_: (jnp.maximum(i - n_p_tiles, 0), 0)),
                  pl.BlockSpec((tc, route.shape[1]), lambda i, *_: (i, 0)),
                  pl.BlockSpec((1, d), lambda i, *_: (0, 0)),
                  pl.BlockSpec(memory_space=pl.ANY)],
        out_specs=[pl.BlockSpec((tc, d), lambda i, *_: (jnp.minimum(i, n_p_tiles - 1), 0)),
                   pl.BlockSpec((tc, d), lambda i, *_: (jnp.maximum(i - n_p_tiles, 0), 0))],
        scratch_shapes=[pltpu.VMEM((2, TOP_K, tc, d), F32), pltpu.SemaphoreType.DMA((2,))],
    )
    return pl.pallas_call(
        functools.partial(_moe_combine_kernel, tc=tc, m=m, n_p_tiles=n_p_tiles),
        grid_spec=grid_spec,
        out_shape=[jax.ShapeDtypeStruct((mp, d), F32), jax.ShapeDtypeStruct((ms, d), F32)],
        compiler_params=_cp("arbitrary"),
        name="moe_combine",
    )(pos, x_p, x_s, route, g_final.reshape(1, d), ys)


def moe_top2(x_p, x_s, p, *, tm=512):
    n_exp = p['w_router'].shape[1]
    x_all = jnp.concatenate([x_p, x_s], axis=0)
    logits = norm_matmul(x_all, p['norm_ffn'][1], _pad_cols(p['w_router']), exact=True)
    route = route_top2(logits, n_exp)
    src, tile_exp, n_used, pos, n_tiles = _routing_tables(route, n_exp, tm)
    ys = moe_experts(x_all, p['norm_ffn'][1], src, tile_exp, n_used, n_tiles,
                     p['w_exp_gate'], p['w_exp_up'], p['w_exp_down'], tm=tm)
    return moe_combine(x_p, x_s, route, pos, ys, p['norm_final'])


def _pad_cols(w, width=LANES):
    return jnp.pad(w, ((0, 0), (0, width - w.shape[1])))


def _split_in1(p, c_width, nh):
    w_main = jnp.concatenate([p['w_in1'][:, :3 * c_width], p['w_in1'][:, 3 * c_width + nh:]], axis=1)
    w_fl = _pad_cols(p['w_in1'][:, 3 * c_width:3 * c_width + nh])
    return w_main, w_fl


def trunk_prompt(x3, mem, p):
    n_seq, t, d = x3.shape
    x = x3.reshape(n_seq * t, d)
    a_width = p['w_conv_a'].shape[1]
    nh = p['b_forget'].shape[0]
    c_width = nh * HEAD_DIM
    n_mem = mem.shape[1]
    xw = p['w_xk'].shape[2]

    mem2 = mem.reshape(n_seq * n_mem, d)
    mk, mv = [], []
    for layer in range(2):
        mk.append(norm_matmul(mem2, p['norm_mem'][layer], p['w_xk'][layer]).reshape(n_seq, n_mem, xw))
        mv.append(norm_matmul(mem2, p['norm_mem'][layer], p['w_xv'][layer]).reshape(n_seq, n_mem, xw))

    def cross(x, layer):
        q = norm_matmul(x, p['norm_cross'][layer], p['w_xq'][layer])
        return matmul_residual([xattn_prompt(q, mk[layer], mv[layer], n_seq, t)], p['w_xo'][layer], x)

    z = norm_matmul(x, p['norm_mix'][0], p['w_in0'])
    ya, h_last, xr_last = lru_prompt(z, n_seq, t, p['w_conv_a'], p['b_conv_a'], p['w_rg_a'], p['b_rg_a'],
                                     p['w_rg_i'], p['b_rg_i'], p['lru_lambda'])
    yb = sgu_prompt(z, n_seq, t, a_width, p['ln_v_g'], p['ln_v_b'], p['w_spatial'], p['b_spatial'])
    x = matmul_residual([ya, yb], p['w_out0'], x)
    x = cross(x, 0)
    x = ffn_swiglu(x, p['norm_ffn'][0], p['w_ffn_gate'], p['w_ffn_up'], p['w_ffn_down'])

    w_main, w_fl = _split_in1(p, c_width, nh)
    zz = norm_matmul(x, p['norm_mix'][1], w_main)
    qkv, gd = zz[:, :3 * c_width], zz[:, 3 * c_width:]
    fl = norm_matmul(x, p['norm_mix'][1], w_fl)[:, :nh]
    fl_t = fl.reshape(n_seq, t, nh).transpose(0, 2, 1)
    lf_t, c_t = logf_cumsum(fl_t, p['b_forget'])
    yc = fox_prompt(qkv, c_t.reshape(n_seq * nh, 1, t), n_seq, t, nh)
    yd, glu_tail = convd_prompt(gd, n_seq, t, p['w_conv_d'], p['b_conv_d'], p['ln_d_g'], p['ln_d_b'])
    x = matmul_residual([yc, yd], p['w_out1'], x)
    x = cross(x, 1)

    width_a = p['w_conv_a'].shape[0]
    width_d = p['w_conv_d'].shape[0]
    return x, dict(
        lru_h=h_last[:, 7, :],
        lru_conv=xr_last[:, 8 - (width_a - 1):, :],
        k=qkv[:, c_width:2 * c_width].reshape(n_seq, t, nh, HEAD_DIM),
        v=qkv[:, 2 * c_width:].reshape(n_seq, t, nh, HEAD_DIM),
        logf=lf_t.transpose(0, 2, 1),
        conv_d=glu_tail[:, 32 - (width_d - 1):, :],
        mem_k=jnp.stack(mk).reshape(2, n_seq, n_mem, xw // HEAD_DIM, HEAD_DIM),
        mem_v=jnp.stack(mv).reshape(2, n_seq, n_mem, xw // HEAD_DIM, HEAD_DIM),
    )


def trunk_sample(x3, mem_k, mem_v, lru_h0, lru_conv0, conv_d0, cache_k, cache_v, cache_lf, page_table, p):
    n, t, d = x3.shape
    assert t == 1
    x = x3.reshape(n, d)
    nh = p['b_forget'].shape[0]
    c_width = nh * HEAD_DIM
    xh = mem_k.shape[3]

    def cross(x, layer):
        q = norm_matmul(x, p['norm_cross'][layer], p['w_xq'][layer], exact=True).reshape(n, xh, HEAD_DIM)
        o = xattn_sample(q, mem_k, mem_v, layer).reshape(n, xh * HEAD_DIM)
        return matmul_residual([o], p['w_xo'][layer], x, exact=True)

    z = norm_matmul(x, p['norm_mix'][0], p['w_in0'], exact=True)
    a_width = lru_h0.shape[1]
    yab, lru_h, chunk_v = mixer0_sample(z, lru_h0, lru_conv0, p['w_conv_a'], p['b_conv_a'], p['w_rg_a'], p['b_rg_a'],
                                        p['w_rg_i'], p['b_rg_i'], p['lru_lambda'], p['ln_v_g'], p['ln_v_b'],
                                        p['w_spatial'], p['b_spatial'])
    x = matmul_residual([yab], p['w_out0'], x, exact=True)
    x = cross(x, 0)
    x = ffn_swiglu(x, p['norm_ffn'][0], p['w_ffn_gate'], p['w_ffn_up'], p['w_ffn_down'], exact=True)

    w_main, w_fl = _split_in1(p, c_width, nh)
    zz = norm_matmul(x, p['norm_mix'][1], w_main, exact=True)
    q, k, v = (zz[:, i * c_width:(i + 1) * c_width].reshape(n, nh, HEAD_DIM) for i in range(3))
    gd = zz[:, 3 * c_width:]
    fl = norm_matmul(x, p['norm_mix'][1], w_fl, exact=True)[:, :nh]
    lf_t, _ = logf_cumsum(fl.T.reshape(1, nh, n), p['b_forget'])
    lf = lf_t.reshape(nh, n).T
    yc = fox_sample(page_table, q, k, v, lf, cache_k, cache_v, cache_lf.transpose(0, 2, 1))
    yd, glu = convd_sample(gd, conv_d0, p['w_conv_d'], p['b_conv_d'], p['ln_d_g'], p['ln_d_b'])
    x = matmul_residual([yc.reshape(n, c_width), yd], p['w_out1'], x, exact=True)
    x = cross(x, 1)

    xr = z[:, a_width:2 * a_width]
    return x, dict(
        lru_h=lru_h,
        lru_conv=jnp.concatenate([lru_conv0[:, 1:], xr[:, None, :]], axis=1),
        chunk_v=chunk_v.reshape(n, 1, -1),
        k=k.reshape(n, 1, nh, HEAD_DIM),
        v=v.reshape(n, 1, nh, HEAD_DIM),
        logf=lf.reshape(n, 1, nh),
        conv_d=jnp.concatenate([conv_d0[:, 1:], glu[:, None, :]], axis=1),
    )


def kernel(x_prompt, x_sample, mem_prompt, state_lru_h, state_lru_conv, cache_fox_k, cache_fox_v, cache_fox_logf, state_conv_d, cache_mem_k, cache_mem_v, page_table, norm_mix, norm_cross, norm_mem, norm_ffn, norm_final, w_in0, w_conv_a, b_conv_a, w_rg_a, b_rg_a, w_rg_i, b_rg_i, lru_lambda, ln_v_g, ln_v_b, w_spatial, b_spatial, w_out0, w_in1, b_forget, w_conv_d, b_conv_d, ln_d_g, ln_d_b, w_out1, w_xq, w_xk, w_xv, w_xo, w_ffn_gate, w_ffn_up, w_ffn_down, w_router, w_exp_gate, w_exp_up, w_exp_down):
    p = dict(norm_mix=norm_mix, norm_cross=norm_cross, norm_mem=norm_mem, norm_ffn=norm_ffn, norm_final=norm_final,
             w_in0=w_in0, w_conv_a=w_conv_a, b_conv_a=b_conv_a, w_rg_a=w_rg_a, b_rg_a=b_rg_a,
             w_rg_i=w_rg_i, b_rg_i=b_rg_i, lru_lambda=lru_lambda, ln_v_g=ln_v_g, ln_v_b=ln_v_b,
             w_spatial=w_spatial, b_spatial=b_spatial, w_out0=w_out0,
             w_in1=w_in1, b_forget=b_forget, w_conv_d=w_conv_d, b_conv_d=b_conv_d,
             ln_d_g=ln_d_g, ln_d_b=ln_d_b, w_out1=w_out1, w_xq=w_xq, w_xk=w_xk, w_xv=w_xv, w_xo=w_xo,
             w_ffn_gate=w_ffn_gate, w_ffn_up=w_ffn_up, w_ffn_down=w_ffn_down,
             w_router=w_router, w_exp_gate=w_exp_gate, w_exp_up=w_exp_up, w_exp_down=w_exp_down)
    x_p, pr = trunk_prompt(x_prompt, mem_prompt, p)
    x_s, sm = trunk_sample(x_sample, cache_mem_k, cache_mem_v, state_lru_h, state_lru_conv, state_conv_d,
                           cache_fox_k, cache_fox_v, cache_fox_logf, page_table, p)
    y_p, y_s = moe_top2(x_p, x_s, p)
    return (y_p.reshape(x_prompt.shape), y_s.reshape(x_sample.shape),
            pr['lru_h'], pr['lru_conv'], pr['k'], pr['v'], pr['logf'], pr['conv_d'], pr['mem_k'], pr['mem_v'],
            sm['lru_h'], sm['lru_conv'], sm['chunk_v'], sm['k'], sm['v'], sm['logf'], sm['conv_d'])
```

```python
import functools

import jax
import jax.numpy as jnp
from jax import lax
from jax.experimental import pallas as pl
from jax.experimental.pallas import tpu as pltpu

F32 = jnp.float32
BF16 = jnp.bfloat16

LANES = 128
HEAD_DIM = 128
CHUNK = 128
LRU_C = 8.0
NORM_EPS = 1e-6
TOP_K = 2
VMEM_LIMIT = 56 << 20


def _cp(*sem, vmem=VMEM_LIMIT):
    return pltpu.CompilerParams(dimension_semantics=sem, vmem_limit_bytes=vmem)


def _tile(n, pref, mult=8):
    if n <= pref:
        return n
    t = (pref // mult) * mult
    while t >= mult:
        if n % t == 0:
            return t
        t -= mult
    return n


def _gelu(x):
    return x * (0.5 * (1.0 + jnp.tanh(0.7978845608028654 * (x + 0.044715 * (x * x * x)))))


def _sigmoid(x):
    return 1.0 / (1.0 + jnp.exp(-x))


def _silu(x):
    return x * _sigmoid(x)


def _log_sigmoid(x):
    return jnp.minimum(x, 0.0) - jnp.log1p(jnp.exp(-jnp.abs(x)))


def _neg_expm1(y):
    return -jnp.tanh(0.5 * y) * (jnp.exp(y) + 1.0)


def _rmsnorm(x, g):
    return x * lax.rsqrt(jnp.mean(x * x, axis=-1, keepdims=True) + NORM_EPS) * g


def _layernorm(x, g, b):
    xc = x - jnp.mean(x, axis=-1, keepdims=True)
    var = jnp.mean(xc * xc, axis=-1, keepdims=True)
    return xc * lax.rsqrt(var + NORM_EPS) * g + b


def _dot(a, b):
    return jnp.dot(a, b, preferred_element_type=F32)


def _dot_t(a, b):
    return lax.dot_general(a, b, (((1,), (1,)), ((), ())), preferred_element_type=F32)


def _mm(a, w, exact):
    if exact:
        return jnp.dot(a.astype(F32), w, preferred_element_type=F32, precision=lax.Precision.HIGHEST)
    return _dot(a.astype(BF16), w.astype(BF16))


def _nmm_kernel(x_ref, g_ref, w_ref, o_ref, h_ref, *, norm, exact):
    @pl.when(pl.program_id(1) == 0)
    def _():
        x = x_ref[...]
        if norm:
            x = _rmsnorm(x, g_ref[...])
        h_ref[...] = x.astype(h_ref.dtype)

    o_ref[...] = _mm(h_ref[...], w_ref[...], exact)


def norm_matmul(x, g, w, *, tm=1024, tn=512, norm=True, exact=False):
    m, k = x.shape
    n = w.shape[1]
    tm, tn = _tile(m, tm), _tile(n, tn, LANES)
    return pl.pallas_call(
        functools.partial(_nmm_kernel, norm=norm, exact=exact),
        grid=(m // tm, n // tn),
        in_specs=[pl.BlockSpec((tm, k), lambda i, j: (i, 0)),
                  pl.BlockSpec((1, k), lambda i, j: (0, 0)),
                  pl.BlockSpec((k, tn), lambda i, j: (0, j))],
        out_specs=pl.BlockSpec((tm, tn), lambda i, j: (i, j)),
        out_shape=jax.ShapeDtypeStruct((m, n), F32),
        scratch_shapes=[pltpu.VMEM((tm, k), F32 if exact else BF16)],
        compiler_params=_cp("parallel", "arbitrary"),
        name="norm_matmul",
    )(x, g.reshape(1, k), w)


def _mmres_kernel(*refs, n_in, exact):
    a_refs, w_refs, r_ref, o_ref = refs[:n_in], refs[n_in:2 * n_in], refs[2 * n_in], refs[2 * n_in + 1]
    acc = r_ref[...]
    for a_ref, w_ref in zip(a_refs, w_refs):
        acc = acc + _mm(a_ref[...], w_ref[...], exact)
    o_ref[...] = acc


def matmul_residual(parts, w, res, *, tm=1024, tn=512, exact=False):
    m, n = res.shape
    tm, tn = _tile(m, tm), _tile(n, tn, LANES)
    kp = parts[0].shape[1]
    assert all(p.shape[1] == kp for p in parts) and kp * len(parts) == w.shape[0]
    n_in = len(parts)
    in_specs = [pl.BlockSpec((tm, kp), lambda i, j: (i, 0)) for _ in parts]
    in_specs += [pl.BlockSpec((kp, tn), functools.partial(lambda i, j, c: (c, j), c=c)) for c in range(n_in)]
    in_specs += [pl.BlockSpec((tm, tn), lambda i, j: (i, j))]
    return pl.pallas_call(
        functools.partial(_mmres_kernel, n_in=n_in, exact=exact),
        grid=(m // tm, n // tn),
        in_specs=in_specs,
        out_specs=pl.BlockSpec((tm, tn), lambda i, j: (i, j)),
        out_shape=jax.ShapeDtypeStruct((m, n), F32),
        compiler_params=_cp("parallel", "arbitrary"),
        name="matmul_residual",
    )(*parts, *([w] * n_in), res)


def _lru_prompt_kernel(gate_ref, xr_ref, wc_ref, bc_ref, wra_ref, bra_ref, wri_ref, bri_ref, lam_ref,
                       ya_ref, hl_ref, xl_ref, xp_ref, *, t, width):
    pad = 8
    xr = xr_ref[...]
    xp_ref[0:pad, :] = jnp.zeros((pad, LANES), F32)
    xp_ref[pad:pad + t, :] = xr
    xc = bc_ref[0]
    for j in range(width):
        xc = xc + wc_ref[0, j:j + 1, :] * xp_ref[pl.ds(pad - (width - 1) + j, t), :]
    xcb = xc.astype(BF16)
    r = _sigmoid(_dot(xcb, wra_ref[0].astype(BF16)) + bra_ref[0])
    ig = _sigmoid(_dot(xcb, wri_ref[0].astype(BF16)) + bri_ref[0])
    log_a = LRU_C * r * _log_sigmoid(lam_ref[0])
    a = jnp.exp(log_a)
    b = jnp.sqrt(_neg_expm1(2.0 * log_a)) * (ig * xc)
    row = lax.broadcasted_iota(jnp.int32, (t, LANES), 0)
    s = 1
    while s < t:
        keep = row >= s
        a_sh = jnp.where(keep, pltpu.roll(a, s, 0), 1.0)
        b_sh = jnp.where(keep, pltpu.roll(b, s, 0), 0.0)
        b = b + a * b_sh
        a = a * a_sh
        s *= 2
    ya_ref[...] = (_gelu(gate_ref[...]) * b).astype(ya_ref.dtype)
    hl_ref[0] = b[t - 8:t, :]
    xl_ref[0] = xr[t - 8:t, :]


def lru_prompt(z, n_seq, t, wc, bc, wra, bra, wri, bri, lam):
    a_width = wc.shape[1]
    nh = a_width // LANES
    width = wc.shape[0]
    wc_h = wc.reshape(width, nh, LANES).transpose(1, 0, 2)
    vec = lambda v: v.reshape(nh, 1, LANES)
    vspec = pl.BlockSpec((1, 1, LANES), lambda n, h: (h, 0, 0))
    mspec = pl.BlockSpec((1, LANES, LANES), lambda n, h: (h, 0, 0))
    return pl.pallas_call(
        functools.partial(_lru_prompt_kernel, t=t, width=width),
        grid=(n_seq, nh),
        in_specs=[pl.BlockSpec((t, LANES), lambda n, h: (n, h)),
                  pl.BlockSpec((t, LANES), lambda n, h, nh=nh: (n, nh + h)),
                  pl.BlockSpec((1, width, LANES), lambda n, h: (h, 0, 0)), vspec,
                  mspec, vspec, mspec, vspec, vspec],
        out_specs=[pl.BlockSpec((t, LANES), lambda n, h: (n, h)),
                   pl.BlockSpec((1, 8, LANES), lambda n, h: (n, 0, h)),
                   pl.BlockSpec((1, 8, LANES), lambda n, h: (n, 0, h))],
        out_shape=[jax.ShapeDtypeStruct((n_seq * t, a_width), BF16),
                   jax.ShapeDtypeStruct((n_seq, 8, a_width), F32),
                   jax.ShapeDtypeStruct((n_seq, 8, a_width), F32)],
        scratch_shapes=[pltpu.VMEM((t + 8, LANES), F32)],
        compiler_params=_cp("parallel", "parallel"),
        name="lru_prompt",
    )(z, z, wc_h, vec(bc), wra, vec(bra), wri, vec(bri), vec(lam))


def _sgu_prompt_kernel(u_ref, v_ref, lng_ref, lnb_ref, ws_ref, bst_ref, yb_ref, *, tt, ng):
    vn = _layernorm(_gelu(v_ref[...]), lng_ref[...], lnb_ref[...])
    ug = _gelu(u_ref[...])
    r = lax.broadcasted_iota(jnp.int32, (CHUNK, CHUNK), 0)
    c = lax.broadcasted_iota(jnp.int32, (CHUNK, CHUNK), 1)
    causal = c <= r
    bst = bst_ref[...]
    for g in range(ng):
        wg = jnp.where(causal, ws_ref[g], 0.0).astype(BF16)
        bias = bst[:, g:g + 1]
        for ch in range(tt // CHUNK):
            rows = slice(ch * CHUNK, (ch + 1) * CHUNK)
            cols = slice(g * LANES, (g + 1) * LANES)
            s = _dot(wg, vn[rows, cols].astype(BF16)) + bias
            yb_ref[rows, cols] = (ug[rows, cols] * s).astype(yb_ref.dtype)


def sgu_prompt(z, n_seq, t, a_width, lng, lnb, ws, bs, *, tt=512):
    m = n_seq * t
    b_width = lng.shape[0]
    ng = b_width // LANES
    tt = _tile(t, tt, CHUNK)
    ucol, vcol = (2 * a_width) // b_width, (2 * a_width) // b_width + 1
    return pl.pallas_call(
        functools.partial(_sgu_prompt_kernel, tt=tt, ng=ng),
        grid=(m // tt,),
        in_specs=[pl.BlockSpec((tt, b_width), lambda i, c=ucol: (i, c)),
                  pl.BlockSpec((tt, b_width), lambda i, c=vcol: (i, c)),
                  pl.BlockSpec((1, b_width), lambda i: (0, 0)),
                  pl.BlockSpec((1, b_width), lambda i: (0, 0)),
                  pl.BlockSpec((ng, CHUNK, CHUNK), lambda i: (0, 0, 0)),
                  pl.BlockSpec((CHUNK, ng), lambda i: (0, 0))],
        out_specs=pl.BlockSpec((tt, b_width), lambda i: (i, 0)),
        out_shape=jax.ShapeDtypeStruct((m, b_width), BF16),
        compiler_params=_cp("parallel"),
        name="sgu_prompt",
    )(z, z, lng.reshape(1, -1), lnb.reshape(1, -1), ws, bs.T)


def _mixer0_sample_kernel(z_ref, h0_ref, cv_ref, wc_ref, bc_ref, wra_ref, bra_ref, wri_ref, bri_ref, lam_ref,
                          lng_ref, lnb_ref, ws0_ref, bs0_ref, yab_ref, h_ref, v_ref, *, a_width, width):
    aw = a_width
    gate, xr, u, v = (z_ref[:, k * aw:(k + 1) * aw] for k in range(4))
    xc = bc_ref[...] + wc_ref[width - 1:width, :] * xr
    for j in range(width - 1):
        xc = xc + wc_ref[j:j + 1, :] * cv_ref[:, j * aw:(j + 1) * aw]
    nh = aw // LANES
    rs, gs = [], []
    for h in range(nh):
        xh = xc[:, h * LANES:(h + 1) * LANES]
        rs.append(_mm(xh, wra_ref[h], True))
        gs.append(_mm(xh, wri_ref[h], True))
    r = _sigmoid(jnp.concatenate(rs, axis=1) + bra_ref[...])
    ig = _sigmoid(jnp.concatenate(gs, axis=1) + bri_ref[...])
    log_a = LRU_C * r * _log_sigmoid(lam_ref[...])
    hs = jnp.exp(log_a) * h0_ref[...] + jnp.sqrt(_neg_expm1(2.0 * log_a)) * (ig * xc)
    h_ref[...] = hs
    vn = _layernorm(_gelu(v), lng_ref[...], lnb_ref[...])
    v_ref[...] = vn
    yab_ref[:, 0:aw] = _gelu(gate) * hs
    yab_ref[:, aw:2 * aw] = _gelu(u) * (ws0_ref[...] * vn + bs0_ref[...])


def mixer0_sample(z, h0, conv0, wc, bc, wra, bra, wri, bri, lam, lng, lnb, ws, bs):
    n, a_width = h0.shape
    width = wc.shape[0]
    row = lambda v: v.reshape(1, -1)
    ws0 = jnp.repeat(ws[:, 0, 0], LANES).reshape(1, -1)
    bs0 = jnp.repeat(bs[:, 0], LANES).reshape(1, -1)
    args = (z, h0, conv0.reshape(n, -1), wc, row(bc), wra, row(bra), wri, row(bri), row(lam),
            row(lng), row(lnb), ws0, bs0)
    return pl.pallas_call(
        functools.partial(_mixer0_sample_kernel, a_width=a_width, width=width),
        out_shape=[jax.ShapeDtypeStruct((n, 2 * a_width), F32),
                   jax.ShapeDtypeStruct((n, a_width), F32),
                   jax.ShapeDtypeStruct((n, a_width), F32)],
        compiler_params=pltpu.CompilerParams(vmem_limit_bytes=VMEM_LIMIT),
        name="mixer0_sample",
    )(*args)


def _logf_kernel(fl_ref, b_ref, lf_ref, c_ref, *, t):
    lf = _log_sigmoid(fl_ref[0] + b_ref[...])
    lf_ref[0] = lf
    lane = lax.broadcasted_iota(jnp.int32, lf.shape, 1)
    c = lf
    s = 1
    while s < t:
        c = c + jnp.where(lane >= s, pltpu.roll(c, s, 1), 0.0)
        s *= 2
    c_ref[0] = c


def logf_cumsum(fl_t, b_forget):
    n, h, t = fl_t.shape
    spec = pl.BlockSpec((1, h, t), lambda i: (i, 0, 0))
    return pl.pallas_call(
        functools.partial(_logf_kernel, t=t),
        grid=(n,),
        in_specs=[spec, pl.BlockSpec((h, 1), lambda i: (0, 0))],
        out_specs=[spec, spec],
        out_shape=[jax.ShapeDtypeStruct((n, h, t), F32)] * 2,
        compiler_params=_cp("parallel"),
        name="logf_cumsum",
    )(fl_t, b_forget.reshape(h, 1))


def _fox_prompt_kernel(q_ref, k_ref, v_ref, ck_ref, o_ref, kb_ref, vb_ref, *, scale, tq, hp):
    qi = pl.program_id(2)

    @pl.when(qi == 0)
    def _():
        kb_ref[...] = k_ref[...].astype(BF16)
        vb_ref[...] = v_ref[...].astype(BF16)

    qb = q_ref[...].astype(BF16)

    def tile(ki, carry, diagonal):
        off = pl.multiple_of(ki * tq, tq)
        out = []
        for hh in range(hp):
            m_prev, l_prev, acc = carry[hh]
            cols = slice(hh * HEAD_DIM, (hh + 1) * HEAD_DIM)
            s = _dot_t(qb[:, cols], kb_ref[pl.ds(off, tq), cols]) * scale - ck_ref[hh, ki]
            if diagonal:
                r = lax.broadcasted_iota(jnp.int32, (tq, tq), 0)
                c = lax.broadcasted_iota(jnp.int32, (tq, tq), 1)
                s = jnp.where(c <= r, s, -jnp.inf)
            m_new = jnp.maximum(m_prev, jnp.max(s, axis=1, keepdims=True))
            alpha = jnp.exp(m_prev - m_new)
            p = jnp.exp(s - m_new)
            l_new = alpha * l_prev + jnp.sum(p, axis=1, keepdims=True)
            acc = alpha * acc + _dot(p.astype(BF16), vb_ref[pl.ds(off, tq), cols])
            out.append((m_new, l_new, acc))
        return tuple(out)

    init = tuple((jnp.full((tq, 1), -jnp.inf, F32), jnp.zeros((tq, 1), F32), jnp.zeros((tq, HEAD_DIM), F32))
                 for _ in range(hp))
    carry = lax.fori_loop(0, qi, lambda ki, c: tile(ki, c, False), init)
    carry = tile(qi, carry, True)
    for hh in range(hp):
        _, l_fin, acc = carry[hh]
        o_ref[:, hh * HEAD_DIM:(hh + 1) * HEAD_DIM] = (acc / l_fin).astype(o_ref.dtype)


def fox_prompt(qkv, c_t, n_seq, t, nh, *, tq=512, hp=2):
    tq = _tile(t, tq, LANES)
    nq = t // tq
    hp = _tile(nh, hp, 1)
    ng = nh // hp
    w = hp * HEAD_DIM
    c4 = c_t.reshape(n_seq * nh, nq, 1, tq)
    return pl.pallas_call(
        functools.partial(_fox_prompt_kernel, scale=HEAD_DIM ** -0.5, tq=tq, hp=hp),
        grid=(n_seq, ng, nq),
        in_specs=[pl.BlockSpec((tq, w), lambda n, g, qi: (n * nq + qi, g)),
                  pl.BlockSpec((t, w), lambda n, g, qi: (n, ng + g)),
                  pl.BlockSpec((t, w), lambda n, g, qi: (n, 2 * ng + g)),
                  pl.BlockSpec((hp, nq, 1, tq), lambda n, g, qi: (n * ng + g, 0, 0, 0))],
        out_specs=pl.BlockSpec((tq, w), lambda n, g, qi: (n * nq + qi, g)),
        out_shape=jax.ShapeDtypeStruct((n_seq * t, nh * HEAD_DIM), BF16),
        scratch_shapes=[pltpu.VMEM((t, w), BF16), pltpu.VMEM((t, w), BF16)],
        compiler_params=_cp("parallel", "parallel", "arbitrary"),
        name="fox_prompt",
    )(qkv, qkv, qkv, c4)


def _fox_sample_kernel(pt_ref, q_ref, kn_ref, vn_ref, lfn_ref, *rest, scale, nh, n_steps, g_pages, page):
    kc = rest[0:g_pages]
    vc = rest[g_pages:2 * g_pages]
    lfc = rest[2 * g_pages:3 * g_pages]
    o_ref, m_ref, l_ref, acc_ref, r_ref = rest[3 * g_pages:]
    step = pl.program_id(1)
    q8 = q_ref[0]

    @pl.when(step == 0)
    def _():
        s_new = jnp.sum(q8 * kn_ref[0], axis=1, keepdims=True) * scale
        m_ref[...] = jnp.broadcast_to(s_new, m_ref.shape)
        l_ref[...] = jnp.ones(l_ref.shape, F32)
        acc_ref[...] = vn_ref[0]
        r_ref[...] = lfn_ref[0]

    m = m_ref[:, 0:1]
    l = l_ref[:, 0:1]
    acc = acc_ref[...]
    run = r_ref[...]
    sel = (lax.broadcasted_iota(jnp.int32, (page, nh, LANES), 0)
           == lax.broadcasted_iota(jnp.int32, (page, nh, LANES), 2))
    tri = (lax.broadcasted_iota(jnp.int32, (page, page), 0)
           >= lax.broadcasted_iota(jnp.int32, (page, page), 1)).astype(F32)
    logits = []
    for g in range(g_pages):
        s3 = jnp.sum(kc[g][0] * q8[None], axis=-1, keepdims=True)
        s_hp = jnp.sum(jnp.where(sel, s3, 0.0), axis=0)
        lf = lfc[g][0]
        suf = _mm(lf, tri, True)
        logits.append(s_hp * scale + (run + suf - lf))
        run = run + suf[:, 0:1]
    top = logits[0]
    for g in range(1, g_pages):
        top = jnp.maximum(top, logits[g])
    m_new = jnp.maximum(m, jnp.max(top, axis=1, keepdims=True))
    alpha = jnp.exp(m - m_new)
    psum = jnp.zeros((nh, page), F32)
    pv = jnp.zeros((nh, LANES), F32)
    for g in range(g_pages):
        p = jnp.exp(logits[g] - m_new)
        psum = psum + p
        p3 = jnp.sum(jnp.where(sel, p[None], 0.0), axis=-1, keepdims=True)
        pv = pv + jnp.sum(p3 * vc[g][0], axis=0)
    l = alpha * l + jnp.sum(psum, axis=1, keepdims=True)
    acc = alpha * acc + pv
    m = m_new
    m_ref[...] = jnp.broadcast_to(m, m_ref.shape)
    l_ref[...] = jnp.broadcast_to(l, l_ref.shape)
    acc_ref[...] = acc
    r_ref[...] = run

    @pl.when(step == n_steps - 1)
    def _():
        o_ref[0] = acc / l


def fox_sample(page_table, q, k_new, v_new, lf_new, cache_k, cache_v, cache_lf_t, *, g_pages=16):
    n, nh, _ = q.shape
    n_pages = page_table.shape[1]
    page = cache_k.shape[1]
    g_pages = _tile(n_pages, g_pages, 1)
    n_steps = n_pages // g_pages
    lf_b = jnp.broadcast_to(lf_new[:, :, None], (n, nh, LANES))
    rspec = pl.BlockSpec((1, nh, LANES), lambda i, s, pt: (i, 0, 0))

    def pidx(g):
        return lambda i, s, pt: (pt[i * n_pages + (n_pages - 1 - (s * g_pages + g))], 0, 0, 0)

    def lidx(g):
        return lambda i, s, pt: (pt[i * n_pages + (n_pages - 1 - (s * g_pages + g))], 0, 0)

    kspecs = [pl.BlockSpec((1, page, nh, LANES), pidx(g)) for g in range(g_pages)]
    lspecs = [pl.BlockSpec((1, nh, page), lidx(g)) for g in range(g_pages)]
    grid_spec = pltpu.PrefetchScalarGridSpec(
        num_scalar_prefetch=1,
        grid=(n, n_steps),
        in_specs=[rspec, rspec, rspec, rspec] + kspecs + kspecs + lspecs,
        out_specs=pl.BlockSpec((1, nh, LANES), lambda i, s, pt: (i, 0, 0)),
        scratch_shapes=[pltpu.VMEM((nh, LANES), F32), pltpu.VMEM((nh, LANES), F32),
                        pltpu.VMEM((nh, LANES), F32), pltpu.VMEM((nh, LANES), F32)],
    )
    return pl.pallas_call(
        functools.partial(_fox_sample_kernel, scale=HEAD_DIM ** -0.5, nh=nh, n_steps=n_steps,
                          g_pages=g_pages, page=page),
        grid_spec=grid_spec,
        out_shape=jax.ShapeDtypeStruct((n, nh, LANES), F32),
        compiler_params=_cp("parallel", "arbitrary"),
        name="fox_sample",
    )(page_table.reshape(-1), q, k_new, v_new, lf_b,
      *([cache_k] * g_pages), *([cache_v] * g_pages), *([cache_lf_t] * g_pages))


def _convd_prompt_kernel(ga_ref, gb_ref, wc_ref, bc_ref, lng_ref, lnb_ref, yd_ref, tail_ref, xp_ref, *, tt, width):
    pad = 32

    @pl.when(pl.program_id(1) == 0)
    def _():
        xp_ref[0:pad, :] = jnp.zeros((pad, xp_ref.shape[1]), F32)

    glu = ga_ref[...] * _sigmoid(gb_ref[...])
    xp_ref[pad:pad + tt, :] = glu
    dc = bc_ref[...]
    for j in range(width):
        dc = dc + wc_ref[j:j + 1, :] * xp_ref[pl.ds(pad - (width - 1) + j, tt), :]
    yd_ref[...] = _silu(_layernorm(dc, lng_ref[...], lnb_ref[...])).astype(yd_ref.dtype)
    tail_ref[0] = glu[tt - pad:tt, :]
    xp_ref[0:pad, :] = glu[tt - pad:tt, :]


def convd_prompt(zz, col0, n_seq, t, wc, bc, lng, lnb, *, tt=512):
    width, d_width = wc.shape
    assert width - 1 <= 32 and col0 % d_width == 0
    cb = col0 // d_width
    tt = _tile(t, tt, 32)
    nt = t // tt
    row = lambda v: v.reshape(1, -1)
    cspec = pl.BlockSpec((1, d_width), lambda n, i: (0, 0))
    return pl.pallas_call(
        functools.partial(_convd_prompt_kernel, tt=tt, width=width),
        grid=(n_seq, nt),
        in_specs=[pl.BlockSpec((tt, d_width), lambda n, i: (n * nt + i, cb)),
                  pl.BlockSpec((tt, d_width), lambda n, i: (n * nt + i, cb + 1)),
                  pl.BlockSpec((width, d_width), lambda n, i: (0, 0)), cspec, cspec, cspec],
        out_specs=[pl.BlockSpec((tt, d_width), lambda n, i: (n * nt + i, 0)),
                   pl.BlockSpec((1, 32, d_width), lambda n, i: (n, 0, 0))],
        out_shape=[jax.ShapeDtypeStruct((n_seq * t, d_width), BF16),
                   jax.ShapeDtypeStruct((n_seq, 32, d_width), F32)],
        scratch_shapes=[pltpu.VMEM((tt + 32, d_width), F32)],
        compiler_params=_cp("parallel", "arbitrary"),
        name="convd_prompt",
    )(zz, zz, wc, row(bc), row(lng), row(lnb))


def _convd_sample_kernel(gd_ref, st_ref, wc_ref, bc_ref, lng_ref, lnb_ref, yd_ref, glu_ref, *, width, d_width):
    glu = gd_ref[:, 0:d_width] * _sigmoid(gd_ref[:, d_width:2 * d_width])
    glu_ref[...] = glu
    dc = bc_ref[...] + wc_ref[width - 1:width, :] * glu
    for j in range(width - 1):
        dc = dc + wc_ref[j:j + 1, :] * st_ref[:, j, :]
    yd_ref[...] = _silu(_layernorm(dc, lng_ref[...], lnb_ref[...]))


def convd_sample(gd, state, wc, bc, lng, lnb, *, tb=16):
    n = gd.shape[0]
    width, d_width = wc.shape
    tb = _tile(n, tb)
    row = lambda v: v.reshape(1, -1)
    cspec = pl.BlockSpec((1, d_width), lambda i: (0, 0))
    return pl.pallas_call(
        functools.partial(_convd_sample_kernel, width=width, d_width=d_width),
        grid=(n // tb,),
        in_specs=[pl.BlockSpec((tb, 2 * d_width), lambda i: (i, 0)),
                  pl.BlockSpec((tb, width - 1, d_width), lambda i: (i, 0, 0)),
                  pl.BlockSpec((width, d_width), lambda i: (0, 0)), cspec, cspec, cspec],
        out_specs=[pl.BlockSpec((tb, d_width), lambda i: (i, 0)),
                   pl.BlockSpec((tb, d_width), lambda i: (i, 0))],
        out_shape=[jax.ShapeDtypeStruct((n, d_width), F32)] * 2,
        compiler_params=_cp("parallel"),
        name="convd_sample",
    )(gd, state, wc, row(bc), row(lng), row(lnb))


def _xattn_prompt_kernel(q_ref, k_ref, v_ref, o_ref, *, nh, scale):
    for h in range(nh):
        cols = slice(h * HEAD_DIM, (h + 1) * HEAD_DIM)
        s = _dot_t(q_ref[:, cols].astype(BF16), k_ref[0, :, cols].astype(BF16)) * scale
        e = jnp.exp(s - jnp.max(s, axis=1, keepdims=True))
        p = e / jnp.sum(e, axis=1, keepdims=True)
        o_ref[:, cols] = _dot(p.astype(BF16), v_ref[0, :, cols].astype(BF16)).astype(o_ref.dtype)


def xattn_prompt(q, mk, mv, n_seq, t, *, tt=512):
    xw = q.shape[1]
    n_mem = mk.shape[1]
    tt = _tile(t, tt, 16)
    nt = t // tt
    kspec = pl.BlockSpec((1, n_mem, xw), lambda n, i: (n, 0, 0))
    return pl.pallas_call(
        functools.partial(_xattn_prompt_kernel, nh=xw // HEAD_DIM, scale=HEAD_DIM ** -0.5),
        grid=(n_seq, nt),
        in_specs=[pl.BlockSpec((tt, xw), lambda n, i: (n * nt + i, 0)), kspec, kspec],
        out_specs=pl.BlockSpec((tt, xw), lambda n, i: (n * nt + i, 0)),
        out_shape=jax.ShapeDtypeStruct((n_seq * t, xw), BF16),
        compiler_params=_cp("parallel", "parallel"),
        name="xattn_prompt",
    )(q, mk, mv)


def _xattn_sample_kernel(q_ref, k_ref, v_ref, o_ref, *, tb, scale):
    for b in range(tb):
        s3 = jnp.sum(k_ref[0, b] * q_ref[b][None], axis=-1, keepdims=True) * scale
        e = jnp.exp(s3 - jnp.max(s3, axis=0, keepdims=True))
        p = e / jnp.sum(e, axis=0, keepdims=True)
        o_ref[b] = jnp.sum(p * v_ref[0, b], axis=0)


def xattn_sample(q, mem_k, mem_v, layer, *, tb=4):
    n, nh, _ = q.shape
    n_mem = mem_k.shape[2]
    tb = _tile(n, tb, 1)
    kspec = pl.BlockSpec((1, tb, n_mem, nh, HEAD_DIM), lambda i: (layer, i, 0, 0, 0))
    qspec = pl.BlockSpec((tb, nh, HEAD_DIM), lambda i: (i, 0, 0))
    return pl.pallas_call(
        functools.partial(_xattn_sample_kernel, tb=tb, scale=HEAD_DIM ** -0.5),
        grid=(n // tb,),
        in_specs=[qspec, kspec, kspec],
        out_specs=qspec,
        out_shape=jax.ShapeDtypeStruct((n, nh, HEAD_DIM), F32),
        compiler_params=_cp("parallel"),
        name="xattn_sample",
    )(q, mem_k, mem_v)


def _ffn_kernel(x_ref, g_ref, wg_ref, wu_ref, wd_ref, o_ref, h_ref, *, exact):
    j = pl.program_id(1)

    @pl.when(j == 0)
    def _():
        x = x_ref[...]
        h_ref[...] = _rmsnorm(x, g_ref[...]).astype(h_ref.dtype)
        o_ref[...] = x

    h = h_ref[...]
    a = _silu(_mm(h, wg_ref[...], exact)) * _mm(h, wu_ref[...], exact)
    o_ref[...] += _mm(a, wd_ref[...], exact)


def ffn_swiglu(x, g, wg, wu, wd, *, tm=1024, tf=256, exact=False):
    m, d = x.shape
    f = wg.shape[1]
    tm, tf = _tile(m, tm), _tile(f, tf, LANES)
    return pl.pallas_call(
        functools.partial(_ffn_kernel, exact=exact),
        grid=(m // tm, f // tf),
        in_specs=[pl.BlockSpec((tm, d), lambda i, j: (i, 0)),
                  pl.BlockSpec((1, d), lambda i, j: (0, 0)),
                  pl.BlockSpec((d, tf), lambda i, j: (0, j)),
                  pl.BlockSpec((d, tf), lambda i, j: (0, j)),
                  pl.BlockSpec((tf, d), lambda i, j: (j, 0))],
        out_specs=pl.BlockSpec((tm, d), lambda i, j: (i, 0)),
        out_shape=jax.ShapeDtypeStruct((m, d), F32),
        scratch_shapes=[pltpu.VMEM((tm, d), F32 if exact else BF16)],
        compiler_params=_cp("parallel", "arbitrary"),
        name="ffn_swiglu",
    )(x, g.reshape(1, d), wg, wu, wd)


def _route_kernel(lg_ref, rt_ref, *, n_exp):
    lg = lg_ref[...]
    lane = lax.broadcasted_iota(jnp.int32, lg.shape, 1)
    neg = jnp.float32(-jnp.inf)
    big = jnp.int32(lg.shape[1])
    lg = jnp.where(lane < n_exp, lg, neg)
    m1 = jnp.max(lg, axis=1, keepdims=True)
    i1 = jnp.min(jnp.where(lg == m1, lane, big), axis=1, keepdims=True)
    rest = jnp.where(lane == i1, neg, lg)
    m2 = jnp.max(rest, axis=1, keepdims=True)
    i2 = jnp.min(jnp.where(rest == m2, lane, big), axis=1, keepdims=True)
    e2 = jnp.exp(m2 - m1)
    g1 = 1.0 / (1.0 + e2)
    g2 = e2 / (1.0 + e2)
    rt_ref[...] = (jnp.where(lane == 0, i1.astype(F32), 0.0) + jnp.where(lane == 1, i2.astype(F32), 0.0)
                   + jnp.where(lane == 2, g1, 0.0) + jnp.where(lane == 3, g2, 0.0))


def route_top2(logits, n_exp, *, tm=512):
    m, w = logits.shape
    tm = _tile(m, tm)
    spec = pl.BlockSpec((tm, w), lambda i: (i, 0))
    return pl.pallas_call(
        functools.partial(_route_kernel, n_exp=n_exp),
        grid=(m // tm,), in_specs=[spec], out_specs=spec,
        out_shape=jax.ShapeDtypeStruct((m, w), F32),
        compiler_params=_cp("parallel"),
        name="route_top2",
    )(logits)


def _routing_tables(route, n_exp, tm):
    m = route.shape[0]
    e = jnp.concatenate([route[:, 0], route[:, 1]]).astype(jnp.int32)
    onehot = (e[:, None] == jnp.arange(n_exp, dtype=jnp.int32)[None, :]).astype(jnp.int32)
    csum = jnp.cumsum(onehot, axis=0)
    rank = jnp.take_along_axis(csum, e[:, None], axis=1)[:, 0] - 1
    counts = csum[-1]
    tiles_per = (counts + tm - 1) // tm
    tile_end = jnp.cumsum(tiles_per)
    n_used = tile_end[-1]
    pos = (tile_end - tiles_per)[e] * tm + rank
    n_tiles = (TOP_K * m + tm - 1) // tm + n_exp
    tok = jnp.concatenate([jnp.arange(m, dtype=jnp.int32)] * TOP_K)
    src = jnp.zeros((n_tiles * tm,), jnp.int32).at[pos].set(tok)
    tile_ids = jnp.minimum(jnp.arange(n_tiles, dtype=jnp.int32), n_used - 1)
    tile_exp = jnp.minimum(jnp.searchsorted(tile_end, tile_ids, side='right'), n_exp - 1).astype(jnp.int32)
    return src, tile_exp, n_used.reshape(1).astype(jnp.int32), pos.astype(jnp.int32), n_tiles


def _moe_expert_kernel(src_ref, texp_ref, nused_ref, x_hbm, g_ref, wg_ref, wu_ref, wd_ref, o_ref,
                       xbuf, h_ref, sem, *, tm):
    t, j = pl.program_id(0), pl.program_id(1)
    n_used = nused_ref[0]
    slot = t % 2

    def row_copy(tile, slot_, i):
        r = src_ref[tile * tm + i]
        return pltpu.make_async_copy(x_hbm.at[pl.ds(r, 1)], xbuf.at[slot_, pl.ds(i, 1)], sem.at[slot_])

    def start_tile(tile, slot_):
        def body(i, c):
            row_copy(tile, slot_, i).start()
            return c
        lax.fori_loop(0, tm, body, 0, unroll=8)

    def wait_tile(tile, slot_):
        def body(i, c):
            row_copy(tile, slot_, i).wait()
            return c
        lax.fori_loop(0, tm, body, 0, unroll=8)

    @pl.when((j == 0) & (t < n_used))
    def _():
        @pl.when(t == 0)
        def _():
            start_tile(0, 0)

        @pl.when(t + 1 < n_used)
        def _():
            start_tile(t + 1, 1 - slot)

        wait_tile(t, slot)
        h_ref[...] = _rmsnorm(xbuf[slot], g_ref[...]).astype(BF16)

    @pl.when(t < n_used)
    def _():
        h = h_ref[...]
        a = _silu(_dot(h, wg_ref[0].astype(BF16))) * _dot(h, wu_ref[0].astype(BF16))
        y = _dot(a.astype(BF16), wd_ref[0].astype(BF16))

        @pl.when(j == 0)
        def _():
            o_ref[...] = y

        @pl.when(j > 0)
        def _():
            o_ref[...] += y

    @pl.when((t >= n_used) & (j == 0))
    def _():
        o_ref[...] = jnp.zeros(o_ref.shape, F32)


def moe_experts(x, g, src, tile_exp, n_used, n_tiles, wg, wu, wd, *, tm, tf=512):
    m, d = x.shape
    f = wg.shape[2]
    tf = _tile(f, tf, LANES)
    nj = f // tf

    def widx(t, j, src_r, texp_r, nused_r):
        return (texp_r[t], 0, jnp.where(t < nused_r[0], j, nj - 1))

    def didx(t, j, src_r, texp_r, nused_r):
        return (texp_r[t], jnp.where(t < nused_r[0], j, nj - 1), 0)

    grid_spec = pltpu.PrefetchScalarGridSpec(
        num_scalar_prefetch=3,
        grid=(n_tiles, nj),
        in_specs=[pl.BlockSpec(memory_space=pl.ANY),
                  pl.BlockSpec((1, d), lambda t, j, *_: (0, 0)),
                  pl.BlockSpec((1, d, tf), widx),
                  pl.BlockSpec((1, d, tf), widx),
                  pl.BlockSpec((1, tf, d), didx)],
        out_specs=pl.BlockSpec((tm, d), lambda t, j, *_: (t, 0)),
        scratch_shapes=[pltpu.VMEM((2, tm, d), F32), pltpu.VMEM((tm, d), BF16), pltpu.SemaphoreType.DMA((2,))],
    )
    return pl.pallas_call(
        functools.partial(_moe_expert_kernel, tm=tm),
        grid_spec=grid_spec,
        out_shape=jax.ShapeDtypeStruct((n_tiles * tm, d), F32),
        compiler_params=_cp("arbitrary", "arbitrary"),
        name="moe_experts",
    )(src, tile_exp, n_used, x, g.reshape(1, d), wg, wu, wd)


def _moe_combine_kernel(pos_ref, xp_ref, xs_ref, rt_ref, gf_ref, ys_hbm, op_ref, os_ref, ybuf, sem,
                        *, tc, m, n_p_tiles):
    i = pl.program_id(0)
    n = pl.num_programs(0)
    slot = i % 2

    def row_copy(tile, slot_, k, r):
        p = pos_ref[k * m + tile * tc + r]
        return pltpu.make_async_copy(ys_hbm.at[pl.ds(p, 1)], ybuf.at[slot_, k, pl.ds(r, 1)], sem.at[slot_])

    def start_tile(tile, slot_):
        def body(r, c):
            for k in range(TOP_K):
                row_copy(tile, slot_, k, r).start()
            return c
        lax.fori_loop(0, tc, body, 0, unroll=4)

    def wait_tile(tile, slot_):
        def body(r, c):
            for k in range(TOP_K):
                row_copy(tile, slot_, k, r).wait()
            return c
        lax.fori_loop(0, tc, body, 0, unroll=4)

    @pl.when(i == 0)
    def _():
        start_tile(0, 0)

    @pl.when(i + 1 < n)
    def _():
        start_tile(i + 1, 1 - slot)

    wait_tile(i, slot)
    rt = rt_ref[...]
    mix = rt[:, 2:3] * ybuf[slot, 0] + rt[:, 3:4] * ybuf[slot, 1]

    @pl.when(i < n_p_tiles)
    def _():
        op_ref[...] = _rmsnorm(xp_ref[...] + mix, gf_ref[...])

    @pl.when(i >= n_p_tiles)
    def _():
        os_ref[...] = _rmsnorm(xs_ref[...] + mix, gf_ref[...])


def moe_combine(x_p, x_s, route, pos, ys, g_final, *, tc=128):
    mp, d = x_p.shape
    ms = x_s.shape[0]
    tc = _tile(ms, tc)
    assert mp % tc == 0
    n_p_tiles, n_s_tiles = mp // tc, ms // tc
    m = mp + ms
    grid_spec = pltpu.PrefetchScalarGridSpec(
        num_scalar_prefetch=1,
        grid=(n_p_tiles + n_s_tiles,),
        in_specs=[pl.BlockSpec((tc, d), lambda i, *_: (jnp.minimum(i, n_p_tiles - 1), 0)),
                  pl.BlockSpec((tc, d), lambda i, *_: (jnp.maximum(i - n_p_tiles, 0), 0)),
                  pl.BlockSpec((tc, route.shape[1]), lambda i, *_: (i, 0)),
                  pl.BlockSpec((1, d), lambda i, *_: (0, 0)),
                  pl.BlockSpec(memory_space=pl.ANY)],
        out_specs=[pl.BlockSpec((tc, d), lambda i, *_: (jnp.minimum(i, n_p_tiles - 1), 0)),
                   pl.BlockSpec((tc, d), lambda i, *_: (jnp.maximum(i - n_p_tiles, 0), 0))],
        scratch_shapes=[pltpu.VMEM((2, TOP_K, tc, d), F32), pltpu.SemaphoreType.DMA((2,))],
    )
    return pl.pallas_call(
        functools.partial(_moe_combine_kernel, tc=tc, m=m, n_p_tiles=n_p_tiles),
        grid_spec=grid_spec,
        out_shape=[jax.ShapeDtypeStruct((mp, d), F32), jax.ShapeDtypeStruct((ms, d), F32)],
        compiler_params=_cp("arbitrary"),
        name="moe_combine",
    )(pos, x_p, x_s, route, g_final.reshape(1, d), ys)


def moe_top2(x_p, x_s, p, *, tm=512):
    n_exp = p['w_router'].shape[1]
    x_all = jnp.concatenate([x_p, x_s], axis=0)
    logits = norm_matmul(x_all, p['norm_ffn'][1], _pad_cols(p['w_router']), exact=True)
    route = route_top2(logits, n_exp)
    src, tile_exp, n_used, pos, n_tiles = _routing_tables(route, n_exp, tm)
    ys = moe_experts(x_all, p['norm_ffn'][1], src, tile_exp, n_used, n_tiles,
                     p['w_exp_gate'], p['w_exp_up'], p['w_exp_down'], tm=tm)
    return moe_combine(x_p, x_s, route, pos, ys, p['norm_final'])


def _pad_cols(w, width=LANES):
    return jnp.pad(w, ((0, 0), (0, width - w.shape[1])))


def _split_in1(p, c_width, nh):
    w_main = jnp.concatenate([p['w_in1'][:, :3 * c_width], p['w_in1'][:, 3 * c_width + nh:]], axis=1)
    w_fl = _pad_cols(p['w_in1'][:, 3 * c_width:3 * c_width + nh])
    return w_main, w_fl


def trunk_prompt(x3, mem, p):
    n_seq, t, d = x3.shape
    x = x3.reshape(n_seq * t, d)
    a_width = p['w_conv_a'].shape[1]
    nh = p['b_forget'].shape[0]
    c_width = nh * HEAD_DIM
    n_mem = mem.shape[1]
    xw = p['w_xk'].shape[2]

    mem2 = mem.reshape(n_seq * n_mem, d)
    mk, mv = [], []
    for layer in range(2):
        mk.append(norm_matmul(mem2, p['norm_mem'][layer], p['w_xk'][layer]).reshape(n_seq, n_mem, xw))
        mv.append(norm_matmul(mem2, p['norm_mem'][layer], p['w_xv'][layer]).reshape(n_seq, n_mem, xw))

    def cross(x, layer):
        q = norm_matmul(x, p['norm_cross'][layer], p['w_xq'][layer])
        return matmul_residual([xattn_prompt(q, mk[layer], mv[layer], n_seq, t)], p['w_xo'][layer], x)

    z = norm_matmul(x, p['norm_mix'][0], p['w_in0'])
    ya, h_last, xr_last = lru_prompt(z, n_seq, t, p['w_conv_a'], p['b_conv_a'], p['w_rg_a'], p['b_rg_a'],
                                     p['w_rg_i'], p['b_rg_i'], p['lru_lambda'])
    yb = sgu_prompt(z, n_seq, t, a_width, p['ln_v_g'], p['ln_v_b'], p['w_spatial'], p['b_spatial'])
    x = matmul_residual([ya, yb], p['w_out0'], x)
    x = cross(x, 0)
    x = ffn_swiglu(x, p['norm_ffn'][0], p['w_ffn_gate'], p['w_ffn_up'], p['w_ffn_down'])

    w_main, w_fl = _split_in1(p, c_width, nh)
    zz = norm_matmul(x, p['norm_mix'][1], w_main)
    qkv = zz
    fl = norm_matmul(x, p['norm_mix'][1], w_fl)[:, :nh]
    fl_t = fl.reshape(n_seq, t, nh).transpose(0, 2, 1)
    lf_t, c_t = logf_cumsum(fl_t, p['b_forget'])
    yc = fox_prompt(zz, c_t, n_seq, t, nh)
    yd, glu_tail = convd_prompt(zz, 3 * c_width, n_seq, t, p['w_conv_d'], p['b_conv_d'], p['ln_d_g'], p['ln_d_b'])
    x = matmul_residual([yc, yd], p['w_out1'], x)
    x = cross(x, 1)

    width_a = p['w_conv_a'].shape[0]
    width_d = p['w_conv_d'].shape[0]
    return x, dict(
        lru_h=h_last[:, 7, :],
        lru_conv=xr_last[:, 8 - (width_a - 1):, :],
        k=qkv[:, c_width:2 * c_width].reshape(n_seq, t, nh, HEAD_DIM),
        v=qkv[:, 2 * c_width:3 * c_width].reshape(n_seq, t, nh, HEAD_DIM),
        logf=lf_t.transpose(0, 2, 1),
        conv_d=glu_tail[:, 32 - (width_d - 1):, :],
        mem_k=jnp.stack(mk).reshape(2, n_seq, n_mem, xw // HEAD_DIM, HEAD_DIM),
        mem_v=jnp.stack(mv).reshape(2, n_seq, n_mem, xw // HEAD_DIM, HEAD_DIM),
    )


def trunk_sample(x3, mem_k, mem_v, lru_h0, lru_conv0, conv_d0, cache_k, cache_v, cache_lf, page_table, p):
    n, t, d = x3.shape
    assert t == 1
    x = x3.reshape(n, d)
    nh = p['b_forget'].shape[0]
    c_width = nh * HEAD_DIM
    xh = mem_k.shape[3]

    def cross(x, layer):
        q = norm_matmul(x, p['norm_cross'][layer], p['w_xq'][layer], exact=True).reshape(n, xh, HEAD_DIM)
        o = xattn_sample(q, mem_k, mem_v, layer).reshape(n, xh * HEAD_DIM)
        return matmul_residual([o], p['w_xo'][layer], x, exact=True)

    z = norm_matmul(x, p['norm_mix'][0], p['w_in0'], exact=True)
    a_width = lru_h0.shape[1]
    yab, lru_h, chunk_v = mixer0_sample(z, lru_h0, lru_conv0, p['w_conv_a'], p['b_conv_a'], p['w_rg_a'], p['b_rg_a'],
                                        p['w_rg_i'], p['b_rg_i'], p['lru_lambda'], p['ln_v_g'], p['ln_v_b'],
                                        p['w_spatial'], p['b_spatial'])
    x = matmul_residual([yab], p['w_out0'], x, exact=True)
    x = cross(x, 0)
    x = ffn_swiglu(x, p['norm_ffn'][0], p['w_ffn_gate'], p['w_ffn_up'], p['w_ffn_down'], exact=True)

    w_main, w_fl = _split_in1(p, c_width, nh)
    zz = norm_matmul(x, p['norm_mix'][1], w_main, exact=True)
    q, k, v = (zz[:, i * c_width:(i + 1) * c_width].reshape(n, nh, HEAD_DIM) for i in range(3))
    gd = zz[:, 3 * c_width:]
    fl = norm_matmul(x, p['norm_mix'][1], w_fl, exact=True)[:, :nh]
    lf_t, _ = logf_cumsum(fl.T.reshape(1, nh, n), p['b_forget'])
    lf = lf_t.reshape(nh, n).T
    yc = fox_sample(page_table, q, k, v, lf, cache_k, cache_v, cache_lf.transpose(0, 2, 1))
    yd, glu = convd_sample(gd, conv_d0, p['w_conv_d'], p['b_conv_d'], p['ln_d_g'], p['ln_d_b'])
    x = matmul_residual([yc.reshape(n, c_width), yd], p['w_out1'], x, exact=True)
    x = cross(x, 1)

    xr = z[:, a_width:2 * a_width]
    return x, dict(
        lru_h=lru_h,
        lru_conv=jnp.concatenate([lru_conv0[:, 1:], xr[:, None, :]], axis=1),
        chunk_v=chunk_v.reshape(n, 1, -1),
        k=k.reshape(n, 1, nh, HEAD_DIM),
        v=v.reshape(n, 1, nh, HEAD_DIM),
        logf=lf.reshape(n, 1, nh),
        conv_d=jnp.concatenate([conv_d0[:, 1:], glu[:, None, :]], axis=1),
    )


def kernel(x_prompt, x_sample, mem_prompt, state_lru_h, state_lru_conv, cache_fox_k, cache_fox_v, cache_fox_logf, state_conv_d, cache_mem_k, cache_mem_v, page_table, norm_mix, norm_cross, norm_mem, norm_ffn, norm_final, w_in0, w_conv_a, b_conv_a, w_rg_a, b_rg_a, w_rg_i, b_rg_i, lru_lambda, ln_v_g, ln_v_b, w_spatial, b_spatial, w_out0, w_in1, b_forget, w_conv_d, b_conv_d, ln_d_g, ln_d_b, w_out1, w_xq, w_xk, w_xv, w_xo, w_ffn_gate, w_ffn_up, w_ffn_down, w_router, w_exp_gate, w_exp_up, w_exp_down):
    p = dict(norm_mix=norm_mix, norm_cross=norm_cross, norm_mem=norm_mem, norm_ffn=norm_ffn, norm_final=norm_final,
             w_in0=w_in0, w_conv_a=w_conv_a, b_conv_a=b_conv_a, w_rg_a=w_rg_a, b_rg_a=b_rg_a,
             w_rg_i=w_rg_i, b_rg_i=b_rg_i, lru_lambda=lru_lambda, ln_v_g=ln_v_g, ln_v_b=ln_v_b,
             w_spatial=w_spatial, b_spatial=b_spatial, w_out0=w_out0,
             w_in1=w_in1, b_forget=b_forget, w_conv_d=w_conv_d, b_conv_d=b_conv_d,
             ln_d_g=ln_d_g, ln_d_b=ln_d_b, w_out1=w_out1, w_xq=w_xq, w_xk=w_xk, w_xv=w_xv, w_xo=w_xo,
             w_ffn_gate=w_ffn_gate, w_ffn_up=w_ffn_up, w_ffn_down=w_ffn_down,
             w_router=w_router, w_exp_gate=w_exp_gate, w_exp_up=w_exp_up, w_exp_down=w_exp_down)
    x_p, pr = trunk_prompt(x_prompt, mem_prompt, p)
    x_s, sm = trunk_sample(x_sample, cache_mem_k, cache_mem_v, state_lru_h, state_lru_conv, state_conv_d,
                           cache_fox_k, cache_fox_v, cache_fox_logf, page_table, p)
    y_p, y_s = moe_top2(x_p, x_s, p)
    return (y_p.reshape(x_prompt.shape), y_s.reshape(x_sample.shape),
            pr['lru_h'], pr['lru_conv'], pr['k'], pr['v'], pr['logf'], pr['conv_d'], pr['mem_k'], pr['mem_v'],
            sm['lru_h'], sm['lru_conv'], sm['chunk_v'], sm['k'], sm['v'], sm['logf'], sm['conv_d'])
```

```python
import functools

import jax
import jax.numpy as jnp
from jax import lax
from jax.experimental import pallas as pl
from jax.experimental.pallas import tpu as pltpu

F32 = jnp.float32
BF16 = jnp.bfloat16

LANES = 128
HEAD_DIM = 128
CHUNK = 128
LRU_C = 8.0
NORM_EPS = 1e-6
TOP_K = 2
VMEM_LIMIT = 56 << 20


def _cp(*sem, vmem=VMEM_LIMIT):
    return pltpu.CompilerParams(dimension_semantics=sem, vmem_limit_bytes=vmem)


def _tile(n, pref, mult=8):
    if n <= pref:
        return n
    t = (pref // mult) * mult
    while t >= mult:
        if n % t == 0:
            return t
        t -= mult
    return n


def _gelu(x):
    return x * (0.5 * (1.0 + jnp.tanh(0.7978845608028654 * (x + 0.044715 * (x * x * x)))))


def _sigmoid(x):
    return 1.0 / (1.0 + jnp.exp(-x))


def _silu(x):
    return x * _sigmoid(x)


def _log_sigmoid(x):
    return jnp.minimum(x, 0.0) - jnp.log1p(jnp.exp(-jnp.abs(x)))


def _neg_expm1(y):
    return -jnp.tanh(0.5 * y) * (jnp.exp(y) + 1.0)


def _rmsnorm(x, g):
    return x * lax.rsqrt(jnp.mean(x * x, axis=-1, keepdims=True) + NORM_EPS) * g


def _layernorm(x, g, b):
    xc = x - jnp.mean(x, axis=-1, keepdims=True)
    var = jnp.mean(xc * xc, axis=-1, keepdims=True)
    return xc * lax.rsqrt(var + NORM_EPS) * g + b


def _dot(a, b):
    return jnp.dot(a, b, preferred_element_type=F32)


def _dot_t(a, b):
    return lax.dot_general(a, b, (((1,), (1,)), ((), ())), preferred_element_type=F32)


def _mm(a, w, exact):
    if exact:
        return jnp.dot(a.astype(F32), w, preferred_element_type=F32, precision=lax.Precision.HIGHEST)
    return _dot(a.astype(BF16), w.astype(BF16))


def _nmm_kernel(x_ref, g_ref, w_ref, o_ref, h_ref, *, norm, exact):
    @pl.when(pl.program_id(1) == 0)
    def _():
        x = x_ref[...]
        if norm:
            x = _rmsnorm(x, g_ref[...])
        h_ref[...] = x.astype(h_ref.dtype)

    o_ref[...] = _mm(h_ref[...], w_ref[...], exact)


def norm_matmul(x, g, w, *, tm=1024, tn=1024, norm=True, exact=False):
    m, k = x.shape
    n = w.shape[1]
    tm, tn = _tile(m, tm), _tile(n, tn, LANES)
    return pl.pallas_call(
        functools.partial(_nmm_kernel, norm=norm, exact=exact),
        grid=(m // tm, n // tn),
        in_specs=[pl.BlockSpec((tm, k), lambda i, j: (i, 0)),
                  pl.BlockSpec((1, k), lambda i, j: (0, 0)),
                  pl.BlockSpec((k, tn), lambda i, j: (0, j))],
        out_specs=pl.BlockSpec((tm, tn), lambda i, j: (i, j)),
        out_shape=jax.ShapeDtypeStruct((m, n), F32),
        scratch_shapes=[pltpu.VMEM((tm, k), F32 if exact else BF16)],
        compiler_params=_cp("parallel", "arbitrary"),
        name="norm_matmul",
    )(x, g.reshape(1, k), w)


def _mmres_kernel(*refs, n_in, exact):
    a_refs, w_refs, r_ref, o_ref = refs[:n_in], refs[n_in:2 * n_in], refs[2 * n_in], refs[2 * n_in + 1]
    acc = r_ref[...]
    for a_ref, w_ref in zip(a_refs, w_refs):
        acc = acc + _mm(a_ref[...], w_ref[...], exact)
    o_ref[...] = acc


def matmul_residual(parts, w, res, *, tm=1024, tn=512, exact=False):
    m, n = res.shape
    tm, tn = _tile(m, tm), _tile(n, tn, LANES)
    kp = parts[0].shape[1]
    assert all(p.shape[1] == kp for p in parts) and kp * len(parts) == w.shape[0]
    n_in = len(parts)
    in_specs = [pl.BlockSpec((tm, kp), lambda i, j: (i, 0)) for _ in parts]
    in_specs += [pl.BlockSpec((kp, tn), functools.partial(lambda i, j, c: (c, j), c=c)) for c in range(n_in)]
    in_specs += [pl.BlockSpec((tm, tn), lambda i, j: (i, j))]
    return pl.pallas_call(
        functools.partial(_mmres_kernel, n_in=n_in, exact=exact),
        grid=(m // tm, n // tn),
        in_specs=in_specs,
        out_specs=pl.BlockSpec((tm, tn), lambda i, j: (i, j)),
        out_shape=jax.ShapeDtypeStruct((m, n), F32),
        compiler_params=_cp("parallel", "arbitrary"),
        name="matmul_residual",
    )(*parts, *([w] * n_in), res)


def _lru_prompt_kernel(gate_ref, xr_ref, wc_ref, bc_ref, wra_ref, bra_ref, wri_ref, bri_ref, lam_ref,
                       ya_ref, hl_ref, xl_ref, xp_ref, *, t, width):
    pad = 8
    xr = xr_ref[...]
    xp_ref[0:pad, :] = jnp.zeros((pad, LANES), F32)
    xp_ref[pad:pad + t, :] = xr
    xc = bc_ref[0]
    for j in range(width):
        xc = xc + wc_ref[0, j:j + 1, :] * xp_ref[pl.ds(pad - (width - 1) + j, t), :]
    xcb = xc.astype(BF16)
    r = _sigmoid(_dot(xcb, wra_ref[0].astype(BF16)) + bra_ref[0])
    ig = _sigmoid(_dot(xcb, wri_ref[0].astype(BF16)) + bri_ref[0])
    log_a = LRU_C * r * _log_sigmoid(lam_ref[0])
    a = jnp.exp(log_a)
    b = jnp.sqrt(_neg_expm1(2.0 * log_a)) * (ig * xc)
    row = lax.broadcasted_iota(jnp.int32, (t, LANES), 0)
    s = 1
    while s < t:
        keep = row >= s
        a_sh = jnp.where(keep, pltpu.roll(a, s, 0), 1.0)
        b_sh = jnp.where(keep, pltpu.roll(b, s, 0), 0.0)
        b = b + a * b_sh
        a = a * a_sh
        s *= 2
    ya_ref[...] = (_gelu(gate_ref[...]) * b).astype(ya_ref.dtype)
    hl_ref[0] = b[t - 8:t, :]
    xl_ref[0] = xr[t - 8:t, :]


def lru_prompt(z, n_seq, t, wc, bc, wra, bra, wri, bri, lam):
    a_width = wc.shape[1]
    nh = a_width // LANES
    width = wc.shape[0]
    wc_h = wc.reshape(width, nh, LANES).transpose(1, 0, 2)
    vec = lambda v: v.reshape(nh, 1, LANES)
    vspec = pl.BlockSpec((1, 1, LANES), lambda n, h: (h, 0, 0))
    mspec = pl.BlockSpec((1, LANES, LANES), lambda n, h: (h, 0, 0))
    return pl.pallas_call(
        functools.partial(_lru_prompt_kernel, t=t, width=width),
        grid=(n_seq, nh),
        in_specs=[pl.BlockSpec((t, LANES), lambda n, h: (n, h)),
                  pl.BlockSpec((t, LANES), lambda n, h, nh=nh: (n, nh + h)),
                  pl.BlockSpec((1, width, LANES), lambda n, h: (h, 0, 0)), vspec,
                  mspec, vspec, mspec, vspec, vspec],
        out_specs=[pl.BlockSpec((t, LANES), lambda n, h: (n, h)),
                   pl.BlockSpec((1, 8, LANES), lambda n, h: (n, 0, h)),
                   pl.BlockSpec((1, 8, LANES), lambda n, h: (n, 0, h))],
        out_shape=[jax.ShapeDtypeStruct((n_seq * t, a_width), BF16),
                   jax.ShapeDtypeStruct((n_seq, 8, a_width), F32),
                   jax.ShapeDtypeStruct((n_seq, 8, a_width), F32)],
        scratch_shapes=[pltpu.VMEM((t + 8, LANES), F32)],
        compiler_params=_cp("parallel", "parallel"),
        name="lru_prompt",
    )(z, z, wc_h, vec(bc), wra, vec(bra), wri, vec(bri), vec(lam))


def _sgu_prompt_kernel(u_ref, v_ref, lng_ref, lnb_ref, ws_ref, bst_ref, yb_ref, *, tt, ng):
    vn = _layernorm(_gelu(v_ref[...]), lng_ref[...], lnb_ref[...])
    ug = _gelu(u_ref[...])
    r = lax.broadcasted_iota(jnp.int32, (CHUNK, CHUNK), 0)
    c = lax.broadcasted_iota(jnp.int32, (CHUNK, CHUNK), 1)
    causal = c <= r
    bst = bst_ref[...]
    for g in range(ng):
        wg = jnp.where(causal, ws_ref[g], 0.0).astype(BF16)
        bias = bst[:, g:g + 1]
        for ch in range(tt // CHUNK):
            rows = slice(ch * CHUNK, (ch + 1) * CHUNK)
            cols = slice(g * LANES, (g + 1) * LANES)
            s = _dot(wg, vn[rows, cols].astype(BF16)) + bias
            yb_ref[rows, cols] = (ug[rows, cols] * s).astype(yb_ref.dtype)


def sgu_prompt(z, n_seq, t, a_width, lng, lnb, ws, bs, *, tt=512):
    m = n_seq * t
    b_width = lng.shape[0]
    ng = b_width // LANES
    tt = _tile(t, tt, CHUNK)
    ucol, vcol = (2 * a_width) // b_width, (2 * a_width) // b_width + 1
    return pl.pallas_call(
        functools.partial(_sgu_prompt_kernel, tt=tt, ng=ng),
        grid=(m // tt,),
        in_specs=[pl.BlockSpec((tt, b_width), lambda i, c=ucol: (i, c)),
                  pl.BlockSpec((tt, b_width), lambda i, c=vcol: (i, c)),
                  pl.BlockSpec((1, b_width), lambda i: (0, 0)),
                  pl.BlockSpec((1, b_width), lambda i: (0, 0)),
                  pl.BlockSpec((ng, CHUNK, CHUNK), lambda i: (0, 0, 0)),
                  pl.BlockSpec((CHUNK, ng), lambda i: (0, 0))],
        out_specs=pl.BlockSpec((tt, b_width), lambda i: (i, 0)),
        out_shape=jax.ShapeDtypeStruct((m, b_width), BF16),
        compiler_params=_cp("parallel"),
        name="sgu_prompt",
    )(z, z, lng.reshape(1, -1), lnb.reshape(1, -1), ws, bs.T)


def _mixer0_sample_kernel(z_ref, h0_ref, cv_ref, wc_ref, bc_ref, wra_ref, bra_ref, wri_ref, bri_ref, lam_ref,
                          lng_ref, lnb_ref, ws0_ref, bs0_ref, yab_ref, h_ref, v_ref, *, a_width, width):
    aw = a_width
    gate, xr, u, v = (z_ref[:, k * aw:(k + 1) * aw] for k in range(4))
    xc = bc_ref[...] + wc_ref[width - 1:width, :] * xr
    for j in range(width - 1):
        xc = xc + wc_ref[j:j + 1, :] * cv_ref[:, j * aw:(j + 1) * aw]
    nh = aw // LANES
    rs, gs = [], []
    for h in range(nh):
        xh = xc[:, h * LANES:(h + 1) * LANES]
        rs.append(_mm(xh, wra_ref[h], True))
        gs.append(_mm(xh, wri_ref[h], True))
    r = _sigmoid(jnp.concatenate(rs, axis=1) + bra_ref[...])
    ig = _sigmoid(jnp.concatenate(gs, axis=1) + bri_ref[...])
    log_a = LRU_C * r * _log_sigmoid(lam_ref[...])
    hs = jnp.exp(log_a) * h0_ref[...] + jnp.sqrt(_neg_expm1(2.0 * log_a)) * (ig * xc)
    h_ref[...] = hs
    vn = _layernorm(_gelu(v), lng_ref[...], lnb_ref[...])
    v_ref[...] = vn
    yab_ref[:, 0:aw] = _gelu(gate) * hs
    yab_ref[:, aw:2 * aw] = _gelu(u) * (ws0_ref[...] * vn + bs0_ref[...])


def mixer0_sample(z, h0, conv0, wc, bc, wra, bra, wri, bri, lam, lng, lnb, ws, bs):
    n, a_width = h0.shape
    width = wc.shape[0]
    row = lambda v: v.reshape(1, -1)
    ws0 = jnp.repeat(ws[:, 0, 0], LANES).reshape(1, -1)
    bs0 = jnp.repeat(bs[:, 0], LANES).reshape(1, -1)
    args = (z, h0, conv0.reshape(n, -1), wc, row(bc), wra, row(bra), wri, row(bri), row(lam),
            row(lng), row(lnb), ws0, bs0)
    return pl.pallas_call(
        functools.partial(_mixer0_sample_kernel, a_width=a_width, width=width),
        out_shape=[jax.ShapeDtypeStruct((n, 2 * a_width), F32),
                   jax.ShapeDtypeStruct((n, a_width), F32),
                   jax.ShapeDtypeStruct((n, a_width), F32)],
        compiler_params=pltpu.CompilerParams(vmem_limit_bytes=VMEM_LIMIT),
        name="mixer0_sample",
    )(*args)


def _logf_kernel(fl_ref, b_ref, lf_ref, c_ref, *, t):
    lf = _log_sigmoid(fl_ref[0] + b_ref[...])
    lf_ref[0] = lf
    lane = lax.broadcasted_iota(jnp.int32, lf.shape, 1)
    c = lf
    s = 1
    while s < t:
        c = c + jnp.where(lane >= s, pltpu.roll(c, s, 1), 0.0)
        s *= 2
    c_ref[0] = c


def logf_cumsum(fl_t, b_forget):
    n, h, t = fl_t.shape
    spec = pl.BlockSpec((1, h, t), lambda i: (i, 0, 0))
    return pl.pallas_call(
        functools.partial(_logf_kernel, t=t),
        grid=(n,),
        in_specs=[spec, pl.BlockSpec((h, 1), lambda i: (0, 0))],
        out_specs=[spec, spec],
        out_shape=[jax.ShapeDtypeStruct((n, h, t), F32)] * 2,
        compiler_params=_cp("parallel"),
        name="logf_cumsum",
    )(fl_t, b_forget.reshape(h, 1))


def _fox_prompt_kernel(q_ref, k_ref, v_ref, ck_ref, o_ref, kb_ref, vb_ref, *, scale, tq, hp):
    qi = pl.program_id(2)

    @pl.when(qi == 0)
    def _():
        kb_ref[...] = k_ref[...].astype(BF16)
        vb_ref[...] = v_ref[...].astype(BF16)

    qb = q_ref[...].astype(BF16)

    def tile(ki, carry, diagonal):
        off = pl.multiple_of(ki * tq, tq)
        out = []
        for hh in range(hp):
            m_prev, l_prev, acc = carry[hh]
            cols = slice(hh * HEAD_DIM, (hh + 1) * HEAD_DIM)
            s = _dot_t(qb[:, cols], kb_ref[pl.ds(off, tq), cols]) * scale - ck_ref[hh, ki]
            if diagonal:
                r = lax.broadcasted_iota(jnp.int32, (tq, tq), 0)
                c = lax.broadcasted_iota(jnp.int32, (tq, tq), 1)
                s = jnp.where(c <= r, s, -jnp.inf)
            m_new = jnp.maximum(m_prev, jnp.max(s, axis=1, keepdims=True))
            alpha = jnp.exp(m_prev - m_new)
            p = jnp.exp(s - m_new)
            l_new = alpha * l_prev + jnp.sum(p, axis=1, keepdims=True)
            acc = alpha * acc + _dot(p.astype(BF16), vb_ref[pl.ds(off, tq), cols])
            out.append((m_new, l_new, acc))
        return tuple(out)

    init = tuple((jnp.full((tq, 1), -jnp.inf, F32), jnp.zeros((tq, 1), F32), jnp.zeros((tq, HEAD_DIM), F32))
                 for _ in range(hp))
    carry = lax.fori_loop(0, qi, lambda ki, c: tile(ki, c, False), init)
    carry = tile(qi, carry, True)
    for hh in range(hp):
        _, l_fin, acc = carry[hh]
        o_ref[:, hh * HEAD_DIM:(hh + 1) * HEAD_DIM] = (acc / l_fin).astype(o_ref.dtype)


def fox_prompt(qkv, c_t, n_seq, t, nh, *, tq=512, hp=2):
    tq = _tile(t, tq, LANES)
    nq = t // tq
    hp = _tile(nh, hp, 1)
    ng = nh // hp
    w = hp * HEAD_DIM
    c4 = c_t.reshape(n_seq * nh, nq, 1, tq)
    return pl.pallas_call(
        functools.partial(_fox_prompt_kernel, scale=HEAD_DIM ** -0.5, tq=tq, hp=hp),
        grid=(n_seq, ng, nq),
        in_specs=[pl.BlockSpec((tq, w), lambda n, g, qi: (n * nq + qi, g)),
                  pl.BlockSpec((t, w), lambda n, g, qi: (n, ng + g)),
                  pl.BlockSpec((t, w), lambda n, g, qi: (n, 2 * ng + g)),
                  pl.BlockSpec((hp, nq, 1, tq), lambda n, g, qi: (n * ng + g, 0, 0, 0))],
        out_specs=pl.BlockSpec((tq, w), lambda n, g, qi: (n * nq + qi, g)),
        out_shape=jax.ShapeDtypeStruct((n_seq * t, nh * HEAD_DIM), BF16),
        scratch_shapes=[pltpu.VMEM((t, w), BF16), pltpu.VMEM((t, w), BF16)],
        compiler_params=_cp("parallel", "parallel", "arbitrary"),
        name="fox_prompt",
    )(qkv, qkv, qkv, c4)


def _fox_sample_kernel(pt_ref, q_ref, kn_ref, vn_ref, lfn_ref, *rest, scale, nh, n_steps, g_pages, page):
    kc = rest[0:g_pages]
    vc = rest[g_pages:2 * g_pages]
    lfc = rest[2 * g_pages:3 * g_pages]
    o_ref, m_ref, l_ref, acc_ref, r_ref = rest[3 * g_pages:]
    step = pl.program_id(1)
    q8 = q_ref[0]

    @pl.when(step == 0)
    def _():
        s_new = jnp.sum(q8 * kn_ref[0], axis=1, keepdims=True) * scale
        m_ref[...] = jnp.broadcast_to(s_new, m_ref.shape)
        l_ref[...] = jnp.ones(l_ref.shape, F32)
        acc_ref[...] = vn_ref[0]
        r_ref[...] = lfn_ref[0]

    m = m_ref[:, 0:1]
    l = l_ref[:, 0:1]
    acc = acc_ref[...]
    run = r_ref[...]
    sel = (lax.broadcasted_iota(jnp.int32, (page, nh, LANES), 0)
           == lax.broadcasted_iota(jnp.int32, (page, nh, LANES), 2))
    tri = (lax.broadcasted_iota(jnp.int32, (page, page), 0)
           >= lax.broadcasted_iota(jnp.int32, (page, page), 1)).astype(F32)
    logits = []
    for g in range(g_pages):
        s3 = jnp.sum(kc[g][0] * q8[None], axis=-1, keepdims=True)
        s_hp = jnp.sum(jnp.where(sel, s3, 0.0), axis=0)
        lf = lfc[g][0]
        suf = _mm(lf, tri, True)
        logits.append(s_hp * scale + (run + suf - lf))
        run = run + suf[:, 0:1]
    top = logits[0]
    for g in range(1, g_pages):
        top = jnp.maximum(top, logits[g])
    m_new = jnp.maximum(m, jnp.max(top, axis=1, keepdims=True))
    alpha = jnp.exp(m - m_new)
    psum = jnp.zeros((nh, page), F32)
    pv = jnp.zeros((nh, LANES), F32)
    for g in range(g_pages):
        p = jnp.exp(logits[g] - m_new)
        psum = psum + p
        p3 = jnp.sum(jnp.where(sel, p[None], 0.0), axis=-1, keepdims=True)
        pv = pv + jnp.sum(p3 * vc[g][0], axis=0)
    l = alpha * l + jnp.sum(psum, axis=1, keepdims=True)
    acc = alpha * acc + pv
    m = m_new
    m_ref[...] = jnp.broadcast_to(m, m_ref.shape)
    l_ref[...] = jnp.broadcast_to(l, l_ref.shape)
    acc_ref[...] = acc
    r_ref[...] = run

    @pl.when(step == n_steps - 1)
    def _():
        o_ref[0] = acc / l


def fox_sample(page_table, q, k_new, v_new, lf_new, cache_k, cache_v, cache_lf_t, *, g_pages=16):
    n, nh, _ = q.shape
    n_pages = page_table.shape[1]
    page = cache_k.shape[1]
    g_pages = _tile(n_pages, g_pages, 1)
    n_steps = n_pages // g_pages
    lf_b = jnp.broadcast_to(lf_new[:, :, None], (n, nh, LANES))
    rspec = pl.BlockSpec((1, nh, LANES), lambda i, s, pt: (i, 0, 0))

    def pidx(g):
        return lambda i, s, pt: (pt[i * n_pages + (n_pages - 1 - (s * g_pages + g))], 0, 0, 0)

    def lidx(g):
        return lambda i, s, pt: (pt[i * n_pages + (n_pages - 1 - (s * g_pages + g))], 0, 0)

    kspecs = [pl.BlockSpec((1, page, nh, LANES), pidx(g)) for g in range(g_pages)]
    lspecs = [pl.BlockSpec((1, nh, page), lidx(g)) for g in range(g_pages)]
    grid_spec = pltpu.PrefetchScalarGridSpec(
        num_scalar_prefetch=1,
        grid=(n, n_steps),
        in_specs=[rspec, rspec, rspec, rspec] + kspecs + kspecs + lspecs,
        out_specs=pl.BlockSpec((1, nh, LANES), lambda i, s, pt: (i, 0, 0)),
        scratch_shapes=[pltpu.VMEM((nh, LANES), F32), pltpu.VMEM((nh, LANES), F32),
                        pltpu.VMEM((nh, LANES), F32), pltpu.VMEM((nh, LANES), F32)],
    )
    return pl.pallas_call(
        functools.partial(_fox_sample_kernel, scale=HEAD_DIM ** -0.5, nh=nh, n_steps=n_steps,
                          g_pages=g_pages, page=page),
        grid_spec=grid_spec,
        out_shape=jax.ShapeDtypeStruct((n, nh, LANES), F32),
        compiler_params=_cp("parallel", "arbitrary"),
        name="fox_sample",
    )(page_table.reshape(-1), q, k_new, v_new, lf_b,
      *([cache_k] * g_pages), *([cache_v] * g_pages), *([cache_lf_t] * g_pages))


def _convd_prompt_kernel(ga_ref, gb_ref, wc_ref, bc_ref, lng_ref, lnb_ref, yd_ref, tail_ref, xp_ref, *, tt, width):
    pad = 32

    @pl.when(pl.program_id(1) == 0)
    def _():
        xp_ref[0:pad, :] = jnp.zeros((pad, xp_ref.shape[1]), F32)

    glu = ga_ref[...] * _sigmoid(gb_ref[...])
    xp_ref[pad:pad + tt, :] = glu
    dc = bc_ref[...]
    for j in range(width):
        dc = dc + wc_ref[j:j + 1, :] * xp_ref[pl.ds(pad - (width - 1) + j, tt), :]
    yd_ref[...] = _silu(_layernorm(dc, lng_ref[...], lnb_ref[...])).astype(yd_ref.dtype)
    tail_ref[0] = glu[tt - pad:tt, :]
    xp_ref[0:pad, :] = glu[tt - pad:tt, :]


def convd_prompt(zz, col0, n_seq, t, wc, bc, lng, lnb, *, tt=512):
    width, d_width = wc.shape
    assert width - 1 <= 32 and col0 % d_width == 0
    cb = col0 // d_width
    tt = _tile(t, tt, 32)
    nt = t // tt
    row = lambda v: v.reshape(1, -1)
    cspec = pl.BlockSpec((1, d_width), lambda n, i: (0, 0))
    return pl.pallas_call(
        functools.partial(_convd_prompt_kernel, tt=tt, width=width),
        grid=(n_seq, nt),
        in_specs=[pl.BlockSpec((tt, d_width), lambda n, i: (n * nt + i, cb)),
                  pl.BlockSpec((tt, d_width), lambda n, i: (n * nt + i, cb + 1)),
                  pl.BlockSpec((width, d_width), lambda n, i: (0, 0)), cspec, cspec, cspec],
        out_specs=[pl.BlockSpec((tt, d_width), lambda n, i: (n * nt + i, 0)),
                   pl.BlockSpec((1, 32, d_width), lambda n, i: (n, 0, 0))],
        out_shape=[jax.ShapeDtypeStruct((n_seq * t, d_width), BF16),
                   jax.ShapeDtypeStruct((n_seq, 32, d_width), F32)],
        scratch_shapes=[pltpu.VMEM((tt + 32, d_width), F32)],
        compiler_params=_cp("parallel", "arbitrary"),
        name="convd_prompt",
    )(zz, zz, wc, row(bc), row(lng), row(lnb))


def _convd_sample_kernel(gd_ref, st_ref, wc_ref, bc_ref, lng_ref, lnb_ref, yd_ref, glu_ref, *, width, d_width):
    glu = gd_ref[:, 0:d_width] * _sigmoid(gd_ref[:, d_width:2 * d_width])
    glu_ref[...] = glu
    dc = bc_ref[...] + wc_ref[width - 1:width, :] * glu
    for j in range(width - 1):
        dc = dc + wc_ref[j:j + 1, :] * st_ref[:, j, :]
    yd_ref[...] = _silu(_layernorm(dc, lng_ref[...], lnb_ref[...]))


def convd_sample(gd, state, wc, bc, lng, lnb, *, tb=16):
    n = gd.shape[0]
    width, d_width = wc.shape
    tb = _tile(n, tb)
    row = lambda v: v.reshape(1, -1)
    cspec = pl.BlockSpec((1, d_width), lambda i: (0, 0))
    return pl.pallas_call(
        functools.partial(_convd_sample_kernel, width=width, d_width=d_width),
        grid=(n // tb,),
        in_specs=[pl.BlockSpec((tb, 2 * d_width), lambda i: (i, 0)),
                  pl.BlockSpec((tb, width - 1, d_width), lambda i: (i, 0, 0)),
                  pl.BlockSpec((width, d_width), lambda i: (0, 0)), cspec, cspec, cspec],
        out_specs=[pl.BlockSpec((tb, d_width), lambda i: (i, 0)),
                   pl.BlockSpec((tb, d_width), lambda i: (i, 0))],
        out_shape=[jax.ShapeDtypeStruct((n, d_width), F32)] * 2,
        compiler_params=_cp("parallel"),
        name="convd_sample",
    )(gd, state, wc, row(bc), row(lng), row(lnb))


def _xattn_prompt_kernel(q_ref, k_ref, v_ref, o_ref, *, nh, scale):
    for h in range(nh):
        cols = slice(h * HEAD_DIM, (h + 1) * HEAD_DIM)
        s = _dot_t(q_ref[:, cols].astype(BF16), k_ref[0, :, cols].astype(BF16)) * scale
        e = jnp.exp(s - jnp.max(s, axis=1, keepdims=True))
        p = e / jnp.sum(e, axis=1, keepdims=True)
        o_ref[:, cols] = _dot(p.astype(BF16), v_ref[0, :, cols].astype(BF16)).astype(o_ref.dtype)


def xattn_prompt(q, mk, mv, n_seq, t, *, tt=512):
    xw = q.shape[1]
    n_mem = mk.shape[1]
    tt = _tile(t, tt, 16)
    nt = t // tt
    kspec = pl.BlockSpec((1, n_mem, xw), lambda n, i: (n, 0, 0))
    return pl.pallas_call(
        functools.partial(_xattn_prompt_kernel, nh=xw // HEAD_DIM, scale=HEAD_DIM ** -0.5),
        grid=(n_seq, nt),
        in_specs=[pl.BlockSpec((tt, xw), lambda n, i: (n * nt + i, 0)), kspec, kspec],
        out_specs=pl.BlockSpec((tt, xw), lambda n, i: (n * nt + i, 0)),
        out_shape=jax.ShapeDtypeStruct((n_seq * t, xw), BF16),
        compiler_params=_cp("parallel", "parallel"),
        name="xattn_prompt",
    )(q, mk, mv)


def _xattn_sample_kernel(q_ref, k_ref, v_ref, o_ref, *, tb, scale):
    for b in range(tb):
        s3 = jnp.sum(k_ref[0, b] * q_ref[b][None], axis=-1, keepdims=True) * scale
        e = jnp.exp(s3 - jnp.max(s3, axis=0, keepdims=True))
        p = e / jnp.sum(e, axis=0, keepdims=True)
        o_ref[b] = jnp.sum(p * v_ref[0, b], axis=0)


def xattn_sample(q, mem_k, mem_v, layer, *, tb=4):
    n, nh, _ = q.shape
    n_mem = mem_k.shape[2]
    tb = _tile(n, tb, 1)
    kspec = pl.BlockSpec((1, tb, n_mem, nh, HEAD_DIM), lambda i: (layer, i, 0, 0, 0))
    qspec = pl.BlockSpec((tb, nh, HEAD_DIM), lambda i: (i, 0, 0))
    return pl.pallas_call(
        functools.partial(_xattn_sample_kernel, tb=tb, scale=HEAD_DIM ** -0.5),
        grid=(n // tb,),
        in_specs=[qspec, kspec, kspec],
        out_specs=qspec,
        out_shape=jax.ShapeDtypeStruct((n, nh, HEAD_DIM), F32),
        compiler_params=_cp("parallel"),
        name="xattn_sample",
    )(q, mem_k, mem_v)


def _ffn_kernel(x_ref, g_ref, wg_ref, wu_ref, wd_ref, o_ref, h_ref, *, exact):
    j = pl.program_id(1)

    @pl.when(j == 0)
    def _():
        x = x_ref[...]
        h_ref[...] = _rmsnorm(x, g_ref[...]).astype(h_ref.dtype)
        o_ref[...] = x

    h = h_ref[...]
    a = _silu(_mm(h, wg_ref[...], exact)) * _mm(h, wu_ref[...], exact)
    o_ref[...] += _mm(a, wd_ref[...], exact)


def ffn_swiglu(x, g, wg, wu, wd, *, tm=1024, tf=256, exact=False):
    m, d = x.shape
    f = wg.shape[1]
    tm, tf = _tile(m, tm), _tile(f, tf, LANES)
    return pl.pallas_call(
        functools.partial(_ffn_kernel, exact=exact),
        grid=(m // tm, f // tf),
        in_specs=[pl.BlockSpec((tm, d), lambda i, j: (i, 0)),
                  pl.BlockSpec((1, d), lambda i, j: (0, 0)),
                  pl.BlockSpec((d, tf), lambda i, j: (0, j)),
                  pl.BlockSpec((d, tf), lambda i, j: (0, j)),
                  pl.BlockSpec((tf, d), lambda i, j: (j, 0))],
        out_specs=pl.BlockSpec((tm, d), lambda i, j: (i, 0)),
        out_shape=jax.ShapeDtypeStruct((m, d), F32),
        scratch_shapes=[pltpu.VMEM((tm, d), F32 if exact else BF16)],
        compiler_params=_cp("parallel", "arbitrary"),
        name="ffn_swiglu",
    )(x, g.reshape(1, d), wg, wu, wd)


def _route_kernel(lg_ref, rt_ref, *, n_exp):
    lg = lg_ref[...]
    lane = lax.broadcasted_iota(jnp.int32, lg.shape, 1)
    neg = jnp.float32(-jnp.inf)
    big = jnp.int32(lg.shape[1])
    lg = jnp.where(lane < n_exp, lg, neg)
    m1 = jnp.max(lg, axis=1, keepdims=True)
    i1 = jnp.min(jnp.where(lg == m1, lane, big), axis=1, keepdims=True)
    rest = jnp.where(lane == i1, neg, lg)
    m2 = jnp.max(rest, axis=1, keepdims=True)
    i2 = jnp.min(jnp.where(rest == m2, lane, big), axis=1, keepdims=True)
    e2 = jnp.exp(m2 - m1)
    g1 = 1.0 / (1.0 + e2)
    g2 = e2 / (1.0 + e2)
    rt_ref[...] = (jnp.where(lane == 0, i1.astype(F32), 0.0) + jnp.where(lane == 1, i2.astype(F32), 0.0)
                   + jnp.where(lane == 2, g1, 0.0) + jnp.where(lane == 3, g2, 0.0))


def route_top2(logits, n_exp, *, tm=512):
    m, w = logits.shape
    tm = _tile(m, tm)
    spec = pl.BlockSpec((tm, w), lambda i: (i, 0))
    return pl.pallas_call(
        functools.partial(_route_kernel, n_exp=n_exp),
        grid=(m // tm,), in_specs=[spec], out_specs=spec,
        out_shape=jax.ShapeDtypeStruct((m, w), F32),
        compiler_params=_cp("parallel"),
        name="route_top2",
    )(logits)


def _routing_tables(route, n_exp, tm):
    m = route.shape[0]
    half = tm // 2
    e = jnp.concatenate([route[:, 0], route[:, 1]]).astype(jnp.int32)
    onehot = (e[:, None] == jnp.arange(n_exp, dtype=jnp.int32)[None, :]).astype(jnp.int32)
    csum = jnp.cumsum(onehot, axis=0)
    rank = jnp.take_along_axis(csum, e[:, None], axis=1)[:, 0] - 1
    counts = csum[-1]
    tiles_per = (counts + tm - 1) // tm
    tile_end = jnp.cumsum(tiles_per)
    tile_start = tile_end - tiles_per
    n_used = tile_end[-1]
    pos = tile_start[e] * tm + rank
    n_tiles = (TOP_K * m + tm - 1) // tm + n_exp
    tok = jnp.concatenate([jnp.arange(m, dtype=jnp.int32)] * TOP_K)
    src = jnp.zeros((n_tiles * tm,), jnp.int32).at[pos].set(tok)
    tile_ids = jnp.minimum(jnp.arange(n_tiles, dtype=jnp.int32), n_used - 1)
    tile_exp = jnp.minimum(jnp.searchsorted(tile_end, tile_ids, side='right'), n_exp - 1).astype(jnp.int32)
    rows_here = counts[tile_exp] - (tile_ids - tile_start[tile_exp]) * tm
    n_live = jnp.where(rows_here > half, tm, half).astype(jnp.int32)
    return src, tile_exp, n_live, n_used.reshape(1).astype(jnp.int32), pos.astype(jnp.int32), n_tiles


def _moe_expert_kernel(src_ref, texp_ref, nlive_ref, nused_ref, x_hbm, g_ref, wg_ref, wu_ref, wd_ref, o_ref,
                       xbuf, h_ref, sem, *, tm):
    t, j = pl.program_id(0), pl.program_id(1)
    n_used = nused_ref[0]
    slot = t % 2
    half = tm // 2
    group = 8

    def row_copy(tile, slot_, i):
        r = src_ref[tile * tm + i]
        return pltpu.make_async_copy(x_hbm.at[pl.ds(r, 1)], xbuf.at[slot_, pl.ds(i, 1)], sem.at[slot_])

    def start_tile(tile, slot_):
        def body(b, c):
            for k in range(group):
                row_copy(tile, slot_, b * group + k).start()
            return c
        lax.fori_loop(0, nlive_ref[tile] // group, body, 0)

    def wait_tile(tile, slot_):
        def body(b, c):
            for k in range(group):
                row_copy(tile, slot_, b * group + k).wait()
            return c
        lax.fori_loop(0, nlive_ref[tile] // group, body, 0)

    live = t < n_used
    full = nlive_ref[t] > half

    @pl.when((j == 0) & live)
    def _():
        @pl.when(t == 0)
        def _():
            start_tile(0, 0)

        @pl.when(t + 1 < n_used)
        def _():
            start_tile(t + 1, 1 - slot)

        wait_tile(t, slot)

        @pl.when(full)
        def _():
            h_ref[...] = _rmsnorm(xbuf[slot], g_ref[...]).astype(BF16)

        @pl.when(jnp.logical_not(full))
        def _():
            h_ref[0:half, :] = _rmsnorm(xbuf[slot, 0:half, :], g_ref[...]).astype(BF16)
            o_ref[half:tm, :] = jnp.zeros((tm - half, o_ref.shape[1]), F32)

    def evaluate(rows):
        h = h_ref[0:rows, :]
        a = _silu(_dot(h, wg_ref[0].astype(BF16))) * _dot(h, wu_ref[0].astype(BF16))
        y = _dot(a.astype(BF16), wd_ref[0].astype(BF16))

        @pl.when(j == 0)
        def _():
            o_ref[0:rows, :] = y

        @pl.when(j > 0)
        def _():
            o_ref[0:rows, :] += y

    @pl.when(live & full)
    def _():
        evaluate(tm)

    @pl.when(live & jnp.logical_not(full))
    def _():
        evaluate(half)

    @pl.when(jnp.logical_not(live) & (j == 0))
    def _():
        o_ref[...] = jnp.zeros(o_ref.shape, F32)


def moe_experts(x, g, src, tile_exp, n_live, n_used, n_tiles, wg, wu, wd, *, tm, tf=256):
    m, d = x.shape
    f = wg.shape[2]
    tf = _tile(f, tf, LANES)
    nj = f // tf

    def widx(t, j, src_r, texp_r, nlive_r, nused_r):
        return (texp_r[t], 0, jnp.where(t < nused_r[0], j, nj - 1))

    def didx(t, j, src_r, texp_r, nlive_r, nused_r):
        return (texp_r[t], jnp.where(t < nused_r[0], j, nj - 1), 0)

    grid_spec = pltpu.PrefetchScalarGridSpec(
        num_scalar_prefetch=4,
        grid=(n_tiles, nj),
        in_specs=[pl.BlockSpec(memory_space=pl.ANY),
                  pl.BlockSpec((1, d), lambda t, j, *_: (0, 0)),
                  pl.BlockSpec((1, d, tf), widx),
                  pl.BlockSpec((1, d, tf), widx),
                  pl.BlockSpec((1, tf, d), didx)],
        out_specs=pl.BlockSpec((tm, d), lambda t, j, *_: (t, 0)),
        scratch_shapes=[pltpu.VMEM((2, tm, d), F32), pltpu.VMEM((tm, d), BF16), pltpu.SemaphoreType.DMA((2,))],
    )
    return pl.pallas_call(
        functools.partial(_moe_expert_kernel, tm=tm),
        grid_spec=grid_spec,
        out_shape=jax.ShapeDtypeStruct((n_tiles * tm, d), F32),
        compiler_params=_cp("arbitrary", "arbitrary"),
        name="moe_experts",
    )(src, tile_exp, n_live, n_used, x, g.reshape(1, d), wg, wu, wd)


def _moe_combine_kernel(pos_ref, xp_ref, xs_ref, rt_ref, gf_ref, ys_hbm, op_ref, os_ref, ybuf, sem,
                        *, tc, m, n_p_tiles):
    i = pl.program_id(0)
    n = pl.num_programs(0)
    slot = i % 2

    def row_copy(tile, slot_, k, r):
        p = pos_ref[k * m + tile * tc + r]
        return pltpu.make_async_copy(ys_hbm.at[pl.ds(p, 1)], ybuf.at[slot_, k, pl.ds(r, 1)], sem.at[slot_])

    def start_tile(tile, slot_):
        def body(r, c):
            for k in range(TOP_K):
                row_copy(tile, slot_, k, r).start()
            return c
        lax.fori_loop(0, tc, body, 0, unroll=4)

    def wait_tile(tile, slot_):
        def body(r, c):
            for k in range(TOP_K):
                row_copy(tile, slot_, k, r).wait()
            return c
        lax.fori_loop(0, tc, body, 0, unroll=4)

    @pl.when(i == 0)
    def _():
        start_tile(0, 0)

    @pl.when(i + 1 < n)
    def _():
        start_tile(i + 1, 1 - slot)

    wait_tile(i, slot)
    rt = rt_ref[...]
    mix = rt[:, 2:3] * ybuf[slot, 0] + rt[:, 3:4] * ybuf[slot, 1]

    @pl.when(i < n_p_tiles)
    def _():
        op_ref[...] = _rmsnorm(xp_ref[...] + mix, gf_ref[...])

    @pl.when(i >= n_p_tiles)
    def _():
        os_ref[...] = _rmsnorm(xs_ref[...] + mix, gf_ref[...])


def moe_combine(x_p, x_s, route, pos, ys, g_final, *, tc=128):
    mp, d = x_p.shape
    ms = x_s.shape[0]
    tc = _tile(ms, tc)
    assert mp % tc == 0
    n_p_tiles, n_s_tiles = mp // tc, ms // tc
    m = mp + ms
    grid_spec = pltpu.PrefetchScalarGridSpec(
        num_scalar_prefetch=1,
        grid=(n_p_tiles + n_s_tiles,),
        in_specs=[pl.BlockSpec((tc, d), lambda i, *_: (jnp.minimum(i, n_p_tiles - 1), 0)),
                  pl.BlockSpec((tc, d), lambda i, *_: (jnp.maximum(i - n_p_tiles, 0), 0)),
                  pl.BlockSpec((tc, route.shape[1]), lambda i, *_: (i, 0)),
                  pl.BlockSpec((1, d), lambda i, *_: (0, 0)),
                  pl.BlockSpec(memory_space=pl.ANY)],
        out_specs=[pl.BlockSpec((tc, d), lambda i, *_: (jnp.minimum(i, n_p_tiles - 1), 0)),
                   pl.BlockSpec((tc, d), lambda i, *_: (jnp.maximum(i - n_p_tiles, 0), 0))],
        scratch_shapes=[pltpu.VMEM((2, TOP_K, tc, d), F32), pltpu.SemaphoreType.DMA((2,))],
    )
    return pl.pallas_call(
        functools.partial(_moe_combine_kernel, tc=tc, m=m, n_p_tiles=n_p_tiles),
        grid_spec=grid_spec,
        out_shape=[jax.ShapeDtypeStruct((mp, d), F32), jax.ShapeDtypeStruct((ms, d), F32)],
        compiler_params=_cp("arbitrary"),
        name="moe_combine",
    )(pos, x_p, x_s, route, g_final.reshape(1, d), ys)


def moe_top2(x_p, x_s, p, *, tm=1024):
    n_exp = p['w_router'].shape[1]
    x_all = jnp.concatenate([x_p, x_s], axis=0)
    logits = norm_matmul(x_all, p['norm_ffn'][1], _pad_cols(p['w_router']), exact=True)
    route = route_top2(logits, n_exp)
    src, tile_exp, n_live, n_used, pos, n_tiles = _routing_tables(route, n_exp, tm)
    ys = moe_experts(x_all, p['norm_ffn'][1], src, tile_exp, n_live, n_used, n_tiles,
                     p['w_exp_gate'], p['w_exp_up'], p['w_exp_down'], tm=tm)
    return moe_combine(x_p, x_s, route, pos, ys, p['norm_final'])


def _pad_cols(w, width=LANES):
    return jnp.pad(w, ((0, 0), (0, width - w.shape[1])))


def _split_in1(p, c_width, nh):
    w_main = jnp.concatenate([p['w_in1'][:, :3 * c_width], p['w_in1'][:, 3 * c_width + nh:]], axis=1)
    w_fl = _pad_cols(p['w_in1'][:, 3 * c_width:3 * c_width + nh])
    return w_main, w_fl


def trunk_prompt(x3, mem, p):
    n_seq, t, d = x3.shape
    x = x3.reshape(n_seq * t, d)
    a_width = p['w_conv_a'].shape[1]
    nh = p['b_forget'].shape[0]
    c_width = nh * HEAD_DIM
    n_mem = mem.shape[1]
    xw = p['w_xk'].shape[2]

    mem2 = mem.reshape(n_seq * n_mem, d)
    mk, mv = [], []
    for layer in range(2):
        mk.append(norm_matmul(mem2, p['norm_mem'][layer], p['w_xk'][layer]).reshape(n_seq, n_mem, xw))
        mv.append(norm_matmul(mem2, p['norm_mem'][layer], p['w_xv'][layer]).reshape(n_seq, n_mem, xw))

    def cross(x, layer):
        q = norm_matmul(x, p['norm_cross'][layer], p['w_xq'][layer])
        return matmul_residual([xattn_prompt(q, mk[layer], mv[layer], n_seq, t)], p['w_xo'][layer], x)

    z = norm_matmul(x, p['norm_mix'][0], p['w_in0'])
    ya, h_last, xr_last = lru_prompt(z, n_seq, t, p['w_conv_a'], p['b_conv_a'], p['w_rg_a'], p['b_rg_a'],
                                     p['w_rg_i'], p['b_rg_i'], p['lru_lambda'])
    yb = sgu_prompt(z, n_seq, t, a_width, p['ln_v_g'], p['ln_v_b'], p['w_spatial'], p['b_spatial'])
    x = matmul_residual([ya, yb], p['w_out0'], x)
    x = cross(x, 0)
    x = ffn_swiglu(x, p['norm_ffn'][0], p['w_ffn_gate'], p['w_ffn_up'], p['w_ffn_down'])

    w_main, w_fl = _split_in1(p, c_width, nh)
    zz = norm_matmul(x, p['norm_mix'][1], w_main)
    qkv = zz
    fl = norm_matmul(x, p['norm_mix'][1], w_fl)[:, :nh]
    fl_t = fl.reshape(n_seq, t, nh).transpose(0, 2, 1)
    lf_t, c_t = logf_cumsum(fl_t, p['b_forget'])
    yc = fox_prompt(zz, c_t, n_seq, t, nh)
    yd, glu_tail = convd_prompt(zz, 3 * c_width, n_seq, t, p['w_conv_d'], p['b_conv_d'], p['ln_d_g'], p['ln_d_b'])
    x = matmul_residual([yc, yd], p['w_out1'], x)
    x = cross(x, 1)

    width_a = p['w_conv_a'].shape[0]
    width_d = p['w_conv_d'].shape[0]
    return x, dict(
        lru_h=h_last[:, 7, :],
        lru_conv=xr_last[:, 8 - (width_a - 1):, :],
        k=qkv[:, c_width:2 * c_width].reshape(n_seq, t, nh, HEAD_DIM),
        v=qkv[:, 2 * c_width:3 * c_width].reshape(n_seq, t, nh, HEAD_DIM),
        logf=lf_t.transpose(0, 2, 1),
        conv_d=glu_tail[:, 32 - (width_d - 1):, :],
        mem_k=jnp.stack(mk).reshape(2, n_seq, n_mem, xw // HEAD_DIM, HEAD_DIM),
        mem_v=jnp.stack(mv).reshape(2, n_seq, n_mem, xw // HEAD_DIM, HEAD_DIM),
    )


def trunk_sample(x3, mem_k, mem_v, lru_h0, lru_conv0, conv_d0, cache_k, cache_v, cache_lf, page_table, p):
    n, t, d = x3.shape
    assert t == 1
    x = x3.reshape(n, d)
    nh = p['b_forget'].shape[0]
    c_width = nh * HEAD_DIM
    xh = mem_k.shape[3]

    def cross(x, layer):
        q = norm_matmul(x, p['norm_cross'][layer], p['w_xq'][layer], exact=True).reshape(n, xh, HEAD_DIM)
        o = xattn_sample(q, mem_k, mem_v, layer).reshape(n, xh * HEAD_DIM)
        return matmul_residual([o], p['w_xo'][layer], x, exact=True)

    z = norm_matmul(x, p['norm_mix'][0], p['w_in0'], exact=True)
    a_width = lru_h0.shape[1]
    yab, lru_h, chunk_v = mixer0_sample(z, lru_h0, lru_conv0, p['w_conv_a'], p['b_conv_a'], p['w_rg_a'], p['b_rg_a'],
                                        p['w_rg_i'], p['b_rg_i'], p['lru_lambda'], p['ln_v_g'], p['ln_v_b'],
                                        p['w_spatial'], p['b_spatial'])
    x = matmul_residual([yab], p['w_out0'], x, exact=True)
    x = cross(x, 0)
    x = ffn_swiglu(x, p['norm_ffn'][0], p['w_ffn_gate'], p['w_ffn_up'], p['w_ffn_down'], exact=True)

    w_main, w_fl = _split_in1(p, c_width, nh)
    zz = norm_matmul(x, p['norm_mix'][1], w_main, exact=True)
    q, k, v = (zz[:, i * c_width:(i + 1) * c_width].reshape(n, nh, HEAD_DIM) for i in range(3))
    gd = zz[:, 3 * c_width:]
    fl = norm_matmul(x, p['norm_mix'][1], w_fl, exact=True)[:, :nh]
    lf_t, _ = logf_cumsum(fl.T.reshape(1, nh, n), p['b_forget'])
    lf = lf_t.reshape(nh, n).T
    yc = fox_sample(page_table, q, k, v, lf, cache_k, cache_v, cache_lf.transpose(0, 2, 1))
    yd, glu = convd_sample(gd, conv_d0, p['w_conv_d'], p['b_conv_d'], p['ln_d_g'], p['ln_d_b'])
    x = matmul_residual([yc.reshape(n, c_width), yd], p['w_out1'], x, exact=True)
    x = cross(x, 1)

    xr = z[:, a_width:2 * a_width]
    return x, dict(
        lru_h=lru_h,
        lru_conv=jnp.concatenate([lru_conv0[:, 1:], xr[:, None, :]], axis=1),
        chunk_v=chunk_v.reshape(n, 1, -1),
        k=k.reshape(n, 1, nh, HEAD_DIM),
        v=v.reshape(n, 1, nh, HEAD_DIM),
        logf=lf.reshape(n, 1, nh),
        conv_d=jnp.concatenate([conv_d0[:, 1:], glu[:, None, :]], axis=1),
    )


def kernel(x_prompt, x_sample, mem_prompt, state_lru_h, state_lru_conv, cache_fox_k, cache_fox_v, cache_fox_logf, state_conv_d, cache_mem_k, cache_mem_v, page_table, norm_mix, norm_cross, norm_mem, norm_ffn, norm_final, w_in0, w_conv_a, b_conv_a, w_rg_a, b_rg_a, w_rg_i, b_rg_i, lru_lambda, ln_v_g, ln_v_b, w_spatial, b_spatial, w_out0, w_in1, b_forget, w_conv_d, b_conv_d, ln_d_g, ln_d_b, w_out1, w_xq, w_xk, w_xv, w_xo, w_ffn_gate, w_ffn_up, w_ffn_down, w_router, w_exp_gate, w_exp_up, w_exp_down):
    p = dict(norm_mix=norm_mix, norm_cross=norm_cross, norm_mem=norm_mem, norm_ffn=norm_ffn, norm_final=norm_final,
             w_in0=w_in0, w_conv_a=w_conv_a, b_conv_a=b_conv_a, w_rg_a=w_rg_a, b_rg_a=b_rg_a,
             w_rg_i=w_rg_i, b_rg_i=b_rg_i, lru_lambda=lru_lambda, ln_v_g=ln_v_g, ln_v_b=ln_v_b,
             w_spatial=w_spatial, b_spatial=b_spatial, w_out0=w_out0,
             w_in1=w_in1, b_forget=b_forget, w_conv_d=w_conv_d, b_conv_d=b_conv_d,
             ln_d_g=ln_d_g, ln_d_b=ln_d_b, w_out1=w_out1, w_xq=w_xq, w_xk=w_xk, w_xv=w_xv, w_xo=w_xo,
             w_ffn_gate=w_ffn_gate, w_ffn_up=w_ffn_up, w_ffn_down=w_ffn_down,
             w_router=w_router, w_exp_gate=w_exp_gate, w_exp_up=w_exp_up, w_exp_down=w_exp_down)
    x_p, pr = trunk_prompt(x_prompt, mem_prompt, p)
    x_s, sm = trunk_sample(x_sample, cache_mem_k, cache_mem_v, state_lru_h, state_lru_conv, state_conv_d,
                           cache_fox_k, cache_fox_v, cache_fox_logf, page_table, p)
    y_p, y_s = moe_top2(x_p, x_s, p)
    return (y_p.reshape(x_prompt.shape), y_s.reshape(x_sample.shape),
            pr['lru_h'], pr['lru_conv'], pr['k'], pr['v'], pr['logf'], pr['conv_d'], pr['mem_k'], pr['mem_v'],
            sm['lru_h'], sm['lru_conv'], sm['chunk_v'], sm['k'], sm['v'], sm['logf'], sm['conv_d'])
```

```python
import functools

import jax
import jax.numpy as jnp
from jax import lax
from jax.experimental import pallas as pl
from jax.experimental.pallas import tpu as pltpu

F32 = jnp.float32
BF16 = jnp.bfloat16

LANES = 128
HEAD_DIM = 128
CHUNK = 128
LRU_C = 8.0
NORM_EPS = 1e-6
TOP_K = 2
VMEM_LIMIT = 56 << 20


def _cp(*sem, vmem=VMEM_LIMIT):
    return pltpu.CompilerParams(dimension_semantics=sem, vmem_limit_bytes=vmem)


def _tile(n, pref, mult=8):
    if n <= pref:
        return n
    t = (pref // mult) * mult
    while t >= mult:
        if n % t == 0:
            return t
        t -= mult
    return n


def _gelu(x):
    return x * (0.5 * (1.0 + jnp.tanh(0.7978845608028654 * (x + 0.044715 * (x * x * x)))))


def _sigmoid(x):
    return 1.0 / (1.0 + jnp.exp(-x))


def _silu(x):
    return x * _sigmoid(x)


def _log_sigmoid(x):
    return jnp.minimum(x, 0.0) - jnp.log1p(jnp.exp(-jnp.abs(x)))


def _neg_expm1(y):
    return -jnp.tanh(0.5 * y) * (jnp.exp(y) + 1.0)


def _rmsnorm(x, g):
    return x * lax.rsqrt(jnp.mean(x * x, axis=-1, keepdims=True) + NORM_EPS) * g


def _layernorm(x, g, b):
    xc = x - jnp.mean(x, axis=-1, keepdims=True)
    var = jnp.mean(xc * xc, axis=-1, keepdims=True)
    return xc * lax.rsqrt(var + NORM_EPS) * g + b


def _dot(a, b):
    return jnp.dot(a, b, preferred_element_type=F32)


def _dot_t(a, b):
    return lax.dot_general(a, b, (((1,), (1,)), ((), ())), preferred_element_type=F32)


def _mm(a, w, exact):
    if exact:
        return jnp.dot(a.astype(F32), w, preferred_element_type=F32, precision=lax.Precision.HIGHEST)
    return _dot(a.astype(BF16), w.astype(BF16))


def _nmm_kernel(x_ref, g_ref, w_ref, o_ref, h_ref, *, norm, exact):
    @pl.when(pl.program_id(1) == 0)
    def _():
        x = x_ref[...]
        if norm:
            x = _rmsnorm(x, g_ref[...])
        h_ref[...] = x.astype(h_ref.dtype)

    o_ref[...] = _mm(h_ref[...], w_ref[...], exact)


def norm_matmul(x, g, w, *, tm=1024, tn=1024, norm=True, exact=False):
    m, k = x.shape
    n = w.shape[1]
    tm, tn = _tile(m, tm), _tile(n, tn, LANES)
    return pl.pallas_call(
        functools.partial(_nmm_kernel, norm=norm, exact=exact),
        grid=(m // tm, n // tn),
        in_specs=[pl.BlockSpec((tm, k), lambda i, j: (i, 0)),
                  pl.BlockSpec((1, k), lambda i, j: (0, 0)),
                  pl.BlockSpec((k, tn), lambda i, j: (0, j))],
        out_specs=pl.BlockSpec((tm, tn), lambda i, j: (i, j)),
        out_shape=jax.ShapeDtypeStruct((m, n), F32),
        scratch_shapes=[pltpu.VMEM((tm, k), F32 if exact else BF16)],
        compiler_params=_cp("parallel", "arbitrary"),
        name="norm_matmul",
    )(x, g.reshape(1, k), w)


def _mmres_kernel(*refs, n_in, exact):
    a_refs, w_refs, r_ref, o_ref = refs[:n_in], refs[n_in:2 * n_in], refs[2 * n_in], refs[2 * n_in + 1]
    acc = r_ref[...]
    for a_ref, w_ref in zip(a_refs, w_refs):
        acc = acc + _mm(a_ref[...], w_ref[...], exact)
    o_ref[...] = acc


def matmul_residual(parts, w, res, *, tm=1024, tn=512, exact=False):
    m, n = res.shape
    tm, tn = _tile(m, tm), _tile(n, tn, LANES)
    kp = parts[0].shape[1]
    assert all(p.shape[1] == kp for p in parts) and kp * len(parts) == w.shape[0]
    n_in = len(parts)
    in_specs = [pl.BlockSpec((tm, kp), lambda i, j: (i, 0)) for _ in parts]
    in_specs += [pl.BlockSpec((kp, tn), functools.partial(lambda i, j, c: (c, j), c=c)) for c in range(n_in)]
    in_specs += [pl.BlockSpec((tm, tn), lambda i, j: (i, j))]
    return pl.pallas_call(
        functools.partial(_mmres_kernel, n_in=n_in, exact=exact),
        grid=(m // tm, n // tn),
        in_specs=in_specs,
        out_specs=pl.BlockSpec((tm, tn), lambda i, j: (i, j)),
        out_shape=jax.ShapeDtypeStruct((m, n), F32),
        compiler_params=_cp("parallel", "arbitrary"),
        name="matmul_residual",
    )(*parts, *([w] * n_in), res)


def _lru_prompt_kernel(gate_ref, xr_ref, wc_ref, bc_ref, wra_ref, bra_ref, wri_ref, bri_ref, lam_ref,
                       ya_ref, hl_ref, xl_ref, xp_ref, *, t, width):
    pad = 8
    xr = xr_ref[...]
    xp_ref[0:pad, :] = jnp.zeros((pad, LANES), F32)
    xp_ref[pad:pad + t, :] = xr
    xc = bc_ref[0]
    for j in range(width):
        xc = xc + wc_ref[0, j:j + 1, :] * xp_ref[pl.ds(pad - (width - 1) + j, t), :]
    xcb = xc.astype(BF16)
    r = _sigmoid(_dot(xcb, wra_ref[0].astype(BF16)) + bra_ref[0])
    ig = _sigmoid(_dot(xcb, wri_ref[0].astype(BF16)) + bri_ref[0])
    log_a = LRU_C * r * _log_sigmoid(lam_ref[0])
    a = jnp.exp(log_a)
    b = jnp.sqrt(_neg_expm1(2.0 * log_a)) * (ig * xc)
    row = lax.broadcasted_iota(jnp.int32, (t, LANES), 0)
    s = 1
    while s < t:
        keep = row >= s
        a_sh = jnp.where(keep, pltpu.roll(a, s, 0), 1.0)
        b_sh = jnp.where(keep, pltpu.roll(b, s, 0), 0.0)
        b = b + a * b_sh
        a = a * a_sh
        s *= 2
    ya_ref[...] = (_gelu(gate_ref[...]) * b).astype(ya_ref.dtype)
    hl_ref[0] = b[t - 8:t, :]
    xl_ref[0] = xr[t - 8:t, :]


def lru_prompt(z, n_seq, t, wc, bc, wra, bra, wri, bri, lam):
    a_width = wc.shape[1]
    nh = a_width // LANES
    width = wc.shape[0]
    wc_h = wc.reshape(width, nh, LANES).transpose(1, 0, 2)
    vec = lambda v: v.reshape(nh, 1, LANES)
    vspec = pl.BlockSpec((1, 1, LANES), lambda n, h: (h, 0, 0))
    mspec = pl.BlockSpec((1, LANES, LANES), lambda n, h: (h, 0, 0))
    return pl.pallas_call(
        functools.partial(_lru_prompt_kernel, t=t, width=width),
        grid=(n_seq, nh),
        in_specs=[pl.BlockSpec((t, LANES), lambda n, h: (n, h)),
                  pl.BlockSpec((t, LANES), lambda n, h, nh=nh: (n, nh + h)),
                  pl.BlockSpec((1, width, LANES), lambda n, h: (h, 0, 0)), vspec,
                  mspec, vspec, mspec, vspec, vspec],
        out_specs=[pl.BlockSpec((t, LANES), lambda n, h: (n, h)),
                   pl.BlockSpec((1, 8, LANES), lambda n, h: (n, 0, h)),
                   pl.BlockSpec((1, 8, LANES), lambda n, h: (n, 0, h))],
        out_shape=[jax.ShapeDtypeStruct((n_seq * t, a_width), BF16),
                   jax.ShapeDtypeStruct((n_seq, 8, a_width), F32),
                   jax.ShapeDtypeStruct((n_seq, 8, a_width), F32)],
        scratch_shapes=[pltpu.VMEM((t + 8, LANES), F32)],
        compiler_params=_cp("parallel", "parallel"),
        name="lru_prompt",
    )(z, z, wc_h, vec(bc), wra, vec(bra), wri, vec(bri), vec(lam))


def _sgu_prompt_kernel(u_ref, v_ref, lng_ref, lnb_ref, ws_ref, bst_ref, yb_ref, *, tt, ng):
    vn = _layernorm(_gelu(v_ref[...]), lng_ref[...], lnb_ref[...])
    ug = _gelu(u_ref[...])
    r = lax.broadcasted_iota(jnp.int32, (CHUNK, CHUNK), 0)
    c = lax.broadcasted_iota(jnp.int32, (CHUNK, CHUNK), 1)
    causal = c <= r
    bst = bst_ref[...]
    for g in range(ng):
        wg = jnp.where(causal, ws_ref[g], 0.0).astype(BF16)
        bias = bst[:, g:g + 1]
        for ch in range(tt // CHUNK):
            rows = slice(ch * CHUNK, (ch + 1) * CHUNK)
            cols = slice(g * LANES, (g + 1) * LANES)
            s = _dot(wg, vn[rows, cols].astype(BF16)) + bias
            yb_ref[rows, cols] = (ug[rows, cols] * s).astype(yb_ref.dtype)


def sgu_prompt(z, n_seq, t, a_width, lng, lnb, ws, bs, *, tt=512):
    m = n_seq * t
    b_width = lng.shape[0]
    ng = b_width // LANES
    tt = _tile(t, tt, CHUNK)
    ucol, vcol = (2 * a_width) // b_width, (2 * a_width) // b_width + 1
    return pl.pallas_call(
        functools.partial(_sgu_prompt_kernel, tt=tt, ng=ng),
        grid=(m // tt,),
        in_specs=[pl.BlockSpec((tt, b_width), lambda i, c=ucol: (i, c)),
                  pl.BlockSpec((tt, b_width), lambda i, c=vcol: (i, c)),
                  pl.BlockSpec((1, b_width), lambda i: (0, 0)),
                  pl.BlockSpec((1, b_width), lambda i: (0, 0)),
                  pl.BlockSpec((ng, CHUNK, CHUNK), lambda i: (0, 0, 0)),
                  pl.BlockSpec((CHUNK, ng), lambda i: (0, 0))],
        out_specs=pl.BlockSpec((tt, b_width), lambda i: (i, 0)),
        out_shape=jax.ShapeDtypeStruct((m, b_width), BF16),
        compiler_params=_cp("parallel"),
        name="sgu_prompt",
    )(z, z, lng.reshape(1, -1), lnb.reshape(1, -1), ws, bs.T)


def _mixer0_sample_kernel(z_ref, h0_ref, cv_ref, wc_ref, bc_ref, wra_ref, bra_ref, wri_ref, bri_ref, lam_ref,
                          lng_ref, lnb_ref, ws0_ref, bs0_ref, yab_ref, h_ref, v_ref, *, a_width, width):
    aw = a_width
    gate, xr, u, v = (z_ref[:, k * aw:(k + 1) * aw] for k in range(4))
    xc = bc_ref[...] + wc_ref[width - 1:width, :] * xr
    for j in range(width - 1):
        xc = xc + wc_ref[j:j + 1, :] * cv_ref[:, j * aw:(j + 1) * aw]
    nh = aw // LANES
    rs, gs = [], []
    for h in range(nh):
        xh = xc[:, h * LANES:(h + 1) * LANES]
        rs.append(_mm(xh, wra_ref[h], True))
        gs.append(_mm(xh, wri_ref[h], True))
    r = _sigmoid(jnp.concatenate(rs, axis=1) + bra_ref[...])
    ig = _sigmoid(jnp.concatenate(gs, axis=1) + bri_ref[...])
    log_a = LRU_C * r * _log_sigmoid(lam_ref[...])
    hs = jnp.exp(log_a) * h0_ref[...] + jnp.sqrt(_neg_expm1(2.0 * log_a)) * (ig * xc)
    h_ref[...] = hs
    vn = _layernorm(_gelu(v), lng_ref[...], lnb_ref[...])
    v_ref[...] = vn
    yab_ref[:, 0:aw] = _gelu(gate) * hs
    yab_ref[:, aw:2 * aw] = _gelu(u) * (ws0_ref[...] * vn + bs0_ref[...])


def mixer0_sample(z, h0, conv0, wc, bc, wra, bra, wri, bri, lam, lng, lnb, ws, bs):
    n, a_width = h0.shape
    width = wc.shape[0]
    row = lambda v: v.reshape(1, -1)
    ws0 = jnp.repeat(ws[:, 0, 0], LANES).reshape(1, -1)
    bs0 = jnp.repeat(bs[:, 0], LANES).reshape(1, -1)
    args = (z, h0, conv0.reshape(n, -1), wc, row(bc), wra, row(bra), wri, row(bri), row(lam),
            row(lng), row(lnb), ws0, bs0)
    return pl.pallas_call(
        functools.partial(_mixer0_sample_kernel, a_width=a_width, width=width),
        out_shape=[jax.ShapeDtypeStruct((n, 2 * a_width), F32),
                   jax.ShapeDtypeStruct((n, a_width), F32),
                   jax.ShapeDtypeStruct((n, a_width), F32)],
        compiler_params=pltpu.CompilerParams(vmem_limit_bytes=VMEM_LIMIT),
        name="mixer0_sample",
    )(*args)


def _logf_kernel(fl_ref, b_ref, lf_ref, c_ref, *, t):
    lf = _log_sigmoid(fl_ref[0] + b_ref[...])
    lf_ref[0] = lf
    lane = lax.broadcasted_iota(jnp.int32, lf.shape, 1)
    c = lf
    s = 1
    while s < t:
        c = c + jnp.where(lane >= s, pltpu.roll(c, s, 1), 0.0)
        s *= 2
    c_ref[0] = c


def logf_cumsum(fl_t, b_forget):
    n, h, t = fl_t.shape
    spec = pl.BlockSpec((1, h, t), lambda i: (i, 0, 0))
    return pl.pallas_call(
        functools.partial(_logf_kernel, t=t),
        grid=(n,),
        in_specs=[spec, pl.BlockSpec((h, 1), lambda i: (0, 0))],
        out_specs=[spec, spec],
        out_shape=[jax.ShapeDtypeStruct((n, h, t), F32)] * 2,
        compiler_params=_cp("parallel"),
        name="logf_cumsum",
    )(fl_t, b_forget.reshape(h, 1))


def _fox_prompt_kernel(q_ref, k_ref, v_ref, ck_ref, o_ref, kb_ref, vb_ref, *, scale, tq, hp):
    qi = pl.program_id(2)

    @pl.when(qi == 0)
    def _():
        kb_ref[...] = k_ref[...].astype(BF16)
        vb_ref[...] = v_ref[...].astype(BF16)

    qb = q_ref[...].astype(BF16)

    def tile(ki, carry, diagonal):
        off = pl.multiple_of(ki * tq, tq)
        out = []
        for hh in range(hp):
            m_prev, l_prev, acc = carry[hh]
            cols = slice(hh * HEAD_DIM, (hh + 1) * HEAD_DIM)
            s = _dot_t(qb[:, cols], kb_ref[pl.ds(off, tq), cols]) * scale - ck_ref[hh, ki]
            if diagonal:
                r = lax.broadcasted_iota(jnp.int32, (tq, tq), 0)
                c = lax.broadcasted_iota(jnp.int32, (tq, tq), 1)
                s = jnp.where(c <= r, s, -jnp.inf)
            m_new = jnp.maximum(m_prev, jnp.max(s, axis=1, keepdims=True))
            alpha = jnp.exp(m_prev - m_new)
            p = jnp.exp(s - m_new)
            l_new = alpha * l_prev + jnp.sum(p, axis=1, keepdims=True)
            acc = alpha * acc + _dot(p.astype(BF16), vb_ref[pl.ds(off, tq), cols])
            out.append((m_new, l_new, acc))
        return tuple(out)

    init = tuple((jnp.full((tq, 1), -jnp.inf, F32), jnp.zeros((tq, 1), F32), jnp.zeros((tq, HEAD_DIM), F32))
                 for _ in range(hp))
    carry = lax.fori_loop(0, qi, lambda ki, c: tile(ki, c, False), init)
    carry = tile(qi, carry, True)
    for hh in range(hp):
        _, l_fin, acc = carry[hh]
        o_ref[:, hh * HEAD_DIM:(hh + 1) * HEAD_DIM] = (acc / l_fin).astype(o_ref.dtype)


def fox_prompt(qkv, c_t, n_seq, t, nh, *, tq=512, hp=2):
    tq = _tile(t, tq, LANES)
    nq = t // tq
    hp = _tile(nh, hp, 1)
    ng = nh // hp
    w = hp * HEAD_DIM
    c4 = c_t.reshape(n_seq * nh, nq, 1, tq)
    return pl.pallas_call(
        functools.partial(_fox_prompt_kernel, scale=HEAD_DIM ** -0.5, tq=tq, hp=hp),
        grid=(n_seq, ng, nq),
        in_specs=[pl.BlockSpec((tq, w), lambda n, g, qi: (n * nq + qi, g)),
                  pl.BlockSpec((t, w), lambda n, g, qi: (n, ng + g)),
                  pl.BlockSpec((t, w), lambda n, g, qi: (n, 2 * ng + g)),
                  pl.BlockSpec((hp, nq, 1, tq), lambda n, g, qi: (n * ng + g, 0, 0, 0))],
        out_specs=pl.BlockSpec((tq, w), lambda n, g, qi: (n * nq + qi, g)),
        out_shape=jax.ShapeDtypeStruct((n_seq * t, nh * HEAD_DIM), BF16),
        scratch_shapes=[pltpu.VMEM((t, w), BF16), pltpu.VMEM((t, w), BF16)],
        compiler_params=_cp("parallel", "parallel", "arbitrary"),
        name="fox_prompt",
    )(qkv, qkv, qkv, c4)


def _fox_sample_kernel(pt_ref, q_ref, kn_ref, vn_ref, lfn_ref, *rest, scale, nh, n_steps, g_pages, page):
    kc = rest[0:g_pages]
    vc = rest[g_pages:2 * g_pages]
    lfc = rest[2 * g_pages:3 * g_pages]
    o_ref, m_ref, l_ref, acc_ref, r_ref = rest[3 * g_pages:]
    step = pl.program_id(1)
    q8 = q_ref[0]

    @pl.when(step == 0)
    def _():
        s_new = jnp.sum(q8 * kn_ref[0], axis=1, keepdims=True) * scale
        m_ref[...] = jnp.broadcast_to(s_new, m_ref.shape)
        l_ref[...] = jnp.ones(l_ref.shape, F32)
        acc_ref[...] = vn_ref[0]
        r_ref[...] = lfn_ref[0]

    m = m_ref[:, 0:1]
    l = l_ref[:, 0:1]
    acc = acc_ref[...]
    run = r_ref[...]
    sel = (lax.broadcasted_iota(jnp.int32, (page, nh, LANES), 0)
           == lax.broadcasted_iota(jnp.int32, (page, nh, LANES), 2))
    tri = (lax.broadcasted_iota(jnp.int32, (page, page), 0)
           >= lax.broadcasted_iota(jnp.int32, (page, page), 1)).astype(F32)
    logits = []
    for g in range(g_pages):
        s3 = jnp.sum(kc[g][0] * q8[None], axis=-1, keepdims=True)
        s_hp = jnp.sum(jnp.where(sel, s3, 0.0), axis=0)
        lf = lfc[g][0]
        suf = _mm(lf, tri, True)
        logits.append(s_hp * scale + (run + suf - lf))
        run = run + suf[:, 0:1]
    top = logits[0]
    for g in range(1, g_pages):
        top = jnp.maximum(top, logits[g])
    m_new = jnp.maximum(m, jnp.max(top, axis=1, keepdims=True))
    alpha = jnp.exp(m - m_new)
    psum = jnp.zeros((nh, page), F32)
    pv = jnp.zeros((nh, LANES), F32)
    for g in range(g_pages):
        p = jnp.exp(logits[g] - m_new)
        psum = psum + p
        p3 = jnp.sum(jnp.where(sel, p[None], 0.0), axis=-1, keepdims=True)
        pv = pv + jnp.sum(p3 * vc[g][0], axis=0)
    l = alpha * l + jnp.sum(psum, axis=1, keepdims=True)
    acc = alpha * acc + pv
    m = m_new
    m_ref[...] = jnp.broadcast_to(m, m_ref.shape)
    l_ref[...] = jnp.broadcast_to(l, l_ref.shape)
    acc_ref[...] = acc
    r_ref[...] = run

    @pl.when(step == n_steps - 1)
    def _():
        o_ref[0] = acc / l


def fox_sample(page_table, q, k_new, v_new, lf_new, cache_k, cache_v, cache_lf_t, *, g_pages=16):
    n, nh, _ = q.shape
    n_pages = page_table.shape[1]
    page = cache_k.shape[1]
    g_pages = _tile(n_pages, g_pages, 1)
    n_steps = n_pages // g_pages
    lf_b = jnp.broadcast_to(lf_new[:, :, None], (n, nh, LANES))
    rspec = pl.BlockSpec((1, nh, LANES), lambda i, s, pt: (i, 0, 0))

    def pidx(g):
        return lambda i, s, pt: (pt[i * n_pages + (n_pages - 1 - (s * g_pages + g))], 0, 0, 0)

    def lidx(g):
        return lambda i, s, pt: (pt[i * n_pages + (n_pages - 1 - (s * g_pages + g))], 0, 0)

    kspecs = [pl.BlockSpec((1, page, nh, LANES), pidx(g)) for g in range(g_pages)]
    lspecs = [pl.BlockSpec((1, nh, page), lidx(g)) for g in range(g_pages)]
    grid_spec = pltpu.PrefetchScalarGridSpec(
        num_scalar_prefetch=1,
        grid=(n, n_steps),
        in_specs=[rspec, rspec, rspec, rspec] + kspecs + kspecs + lspecs,
        out_specs=pl.BlockSpec((1, nh, LANES), lambda i, s, pt: (i, 0, 0)),
        scratch_shapes=[pltpu.VMEM((nh, LANES), F32), pltpu.VMEM((nh, LANES), F32),
                        pltpu.VMEM((nh, LANES), F32), pltpu.VMEM((nh, LANES), F32)],
    )
    return pl.pallas_call(
        functools.partial(_fox_sample_kernel, scale=HEAD_DIM ** -0.5, nh=nh, n_steps=n_steps,
                          g_pages=g_pages, page=page),
        grid_spec=grid_spec,
        out_shape=jax.ShapeDtypeStruct((n, nh, LANES), F32),
        compiler_params=_cp("parallel", "arbitrary"),
        name="fox_sample",
    )(page_table.reshape(-1), q, k_new, v_new, lf_b,
      *([cache_k] * g_pages), *([cache_v] * g_pages), *([cache_lf_t] * g_pages))


def _convd_prompt_kernel(ga_ref, gb_ref, wc_ref, bc_ref, lng_ref, lnb_ref, yd_ref, tail_ref, xp_ref, *, tt, width):
    pad = 32

    @pl.when(pl.program_id(1) == 0)
    def _():
        xp_ref[0:pad, :] = jnp.zeros((pad, xp_ref.shape[1]), F32)

    glu = ga_ref[...] * _sigmoid(gb_ref[...])
    xp_ref[pad:pad + tt, :] = glu
    dc = bc_ref[...]
    for j in range(width):
        dc = dc + wc_ref[j:j + 1, :] * xp_ref[pl.ds(pad - (width - 1) + j, tt), :]
    yd_ref[...] = _silu(_layernorm(dc, lng_ref[...], lnb_ref[...])).astype(yd_ref.dtype)
    tail_ref[0] = glu[tt - pad:tt, :]
    xp_ref[0:pad, :] = glu[tt - pad:tt, :]


def convd_prompt(zz, col0, n_seq, t, wc, bc, lng, lnb, *, tt=512):
    width, d_width = wc.shape
    assert width - 1 <= 32 and col0 % d_width == 0
    cb = col0 // d_width
    tt = _tile(t, tt, 32)
    nt = t // tt
    row = lambda v: v.reshape(1, -1)
    cspec = pl.BlockSpec((1, d_width), lambda n, i: (0, 0))
    return pl.pallas_call(
        functools.partial(_convd_prompt_kernel, tt=tt, width=width),
        grid=(n_seq, nt),
        in_specs=[pl.BlockSpec((tt, d_width), lambda n, i: (n * nt + i, cb)),
                  pl.BlockSpec((tt, d_width), lambda n, i: (n * nt + i, cb + 1)),
                  pl.BlockSpec((width, d_width), lambda n, i: (0, 0)), cspec, cspec, cspec],
        out_specs=[pl.BlockSpec((tt, d_width), lambda n, i: (n * nt + i, 0)),
                   pl.BlockSpec((1, 32, d_width), lambda n, i: (n, 0, 0))],
        out_shape=[jax.ShapeDtypeStruct((n_seq * t, d_width), BF16),
                   jax.ShapeDtypeStruct((n_seq, 32, d_width), F32)],
        scratch_shapes=[pltpu.VMEM((tt + 32, d_width), F32)],
        compiler_params=_cp("parallel", "arbitrary"),
        name="convd_prompt",
    )(zz, zz, wc, row(bc), row(lng), row(lnb))


def _convd_sample_kernel(gd_ref, st_ref, wc_ref, bc_ref, lng_ref, lnb_ref, yd_ref, glu_ref, *, width, d_width):
    glu = gd_ref[:, 0:d_width] * _sigmoid(gd_ref[:, d_width:2 * d_width])
    glu_ref[...] = glu
    dc = bc_ref[...] + wc_ref[width - 1:width, :] * glu
    for j in range(width - 1):
        dc = dc + wc_ref[j:j + 1, :] * st_ref[:, j, :]
    yd_ref[...] = _silu(_layernorm(dc, lng_ref[...], lnb_ref[...]))


def convd_sample(gd, state, wc, bc, lng, lnb, *, tb=16):
    n = gd.shape[0]
    width, d_width = wc.shape
    tb = _tile(n, tb)
    row = lambda v: v.reshape(1, -1)
    cspec = pl.BlockSpec((1, d_width), lambda i: (0, 0))
    return pl.pallas_call(
        functools.partial(_convd_sample_kernel, width=width, d_width=d_width),
        grid=(n // tb,),
        in_specs=[pl.BlockSpec((tb, 2 * d_width), lambda i: (i, 0)),
                  pl.BlockSpec((tb, width - 1, d_width), lambda i: (i, 0, 0)),
                  pl.BlockSpec((width, d_width), lambda i: (0, 0)), cspec, cspec, cspec],
        out_specs=[pl.BlockSpec((tb, d_width), lambda i: (i, 0)),
                   pl.BlockSpec((tb, d_width), lambda i: (i, 0))],
        out_shape=[jax.ShapeDtypeStruct((n, d_width), F32)] * 2,
        compiler_params=_cp("parallel"),
        name="convd_sample",
    )(gd, state, wc, row(bc), row(lng), row(lnb))


def _xattn_prompt_kernel(q_ref, k_ref, v_ref, o_ref, *, nh, scale):
    for h in range(nh):
        cols = slice(h * HEAD_DIM, (h + 1) * HEAD_DIM)
        s = _dot_t(q_ref[:, cols].astype(BF16), k_ref[0, :, cols].astype(BF16)) * scale
        e = jnp.exp(s - jnp.max(s, axis=1, keepdims=True))
        p = e / jnp.sum(e, axis=1, keepdims=True)
        o_ref[:, cols] = _dot(p.astype(BF16), v_ref[0, :, cols].astype(BF16)).astype(o_ref.dtype)


def xattn_prompt(q, mk, mv, n_seq, t, *, tt=512):
    xw = q.shape[1]
    n_mem = mk.shape[1]
    tt = _tile(t, tt, 16)
    nt = t // tt
    kspec = pl.BlockSpec((1, n_mem, xw), lambda n, i: (n, 0, 0))
    return pl.pallas_call(
        functools.partial(_xattn_prompt_kernel, nh=xw // HEAD_DIM, scale=HEAD_DIM ** -0.5),
        grid=(n_seq, nt),
        in_specs=[pl.BlockSpec((tt, xw), lambda n, i: (n * nt + i, 0)), kspec, kspec],
        out_specs=pl.BlockSpec((tt, xw), lambda n, i: (n * nt + i, 0)),
        out_shape=jax.ShapeDtypeStruct((n_seq * t, xw), BF16),
        compiler_params=_cp("parallel", "parallel"),
        name="xattn_prompt",
    )(q, mk, mv)


def _xattn_sample_kernel(q_ref, k_ref, v_ref, o_ref, *, tb, scale):
    for b in range(tb):
        s3 = jnp.sum(k_ref[0, b] * q_ref[b][None], axis=-1, keepdims=True) * scale
        e = jnp.exp(s3 - jnp.max(s3, axis=0, keepdims=True))
        p = e / jnp.sum(e, axis=0, keepdims=True)
        o_ref[b] = jnp.sum(p * v_ref[0, b], axis=0)


def xattn_sample(q, mem_k, mem_v, layer, *, tb=4):
    n, nh, _ = q.shape
    n_mem = mem_k.shape[2]
    tb = _tile(n, tb, 1)
    kspec = pl.BlockSpec((1, tb, n_mem, nh, HEAD_DIM), lambda i: (layer, i, 0, 0, 0))
    qspec = pl.BlockSpec((tb, nh, HEAD_DIM), lambda i: (i, 0, 0))
    return pl.pallas_call(
        functools.partial(_xattn_sample_kernel, tb=tb, scale=HEAD_DIM ** -0.5),
        grid=(n // tb,),
        in_specs=[qspec, kspec, kspec],
        out_specs=qspec,
        out_shape=jax.ShapeDtypeStruct((n, nh, HEAD_DIM), F32),
        compiler_params=_cp("parallel"),
        name="xattn_sample",
    )(q, mem_k, mem_v)


def _ffn_kernel(x_ref, g_ref, wg_ref, wu_ref, wd_ref, o_ref, h_ref, *, exact):
    j = pl.program_id(1)

    @pl.when(j == 0)
    def _():
        x = x_ref[...]
        h_ref[...] = _rmsnorm(x, g_ref[...]).astype(h_ref.dtype)
        o_ref[...] = x

    h = h_ref[...]
    a = _silu(_mm(h, wg_ref[...], exact)) * _mm(h, wu_ref[...], exact)
    o_ref[...] += _mm(a, wd_ref[...], exact)


def ffn_swiglu(x, g, wg, wu, wd, *, tm=1024, tf=256, exact=False):
    m, d = x.shape
    f = wg.shape[1]
    tm, tf = _tile(m, tm), _tile(f, tf, LANES)
    return pl.pallas_call(
        functools.partial(_ffn_kernel, exact=exact),
        grid=(m // tm, f // tf),
        in_specs=[pl.BlockSpec((tm, d), lambda i, j: (i, 0)),
                  pl.BlockSpec((1, d), lambda i, j: (0, 0)),
                  pl.BlockSpec((d, tf), lambda i, j: (0, j)),
                  pl.BlockSpec((d, tf), lambda i, j: (0, j)),
                  pl.BlockSpec((tf, d), lambda i, j: (j, 0))],
        out_specs=pl.BlockSpec((tm, d), lambda i, j: (i, 0)),
        out_shape=jax.ShapeDtypeStruct((m, d), F32),
        scratch_shapes=[pltpu.VMEM((tm, d), F32 if exact else BF16)],
        compiler_params=_cp("parallel", "arbitrary"),
        name="ffn_swiglu",
    )(x, g.reshape(1, d), wg, wu, wd)


def _route_kernel(lg_ref, rt_ref, *, n_exp):
    lg = lg_ref[...]
    lane = lax.broadcasted_iota(jnp.int32, lg.shape, 1)
    neg = jnp.float32(-jnp.inf)
    big = jnp.int32(lg.shape[1])
    lg = jnp.where(lane < n_exp, lg, neg)
    m1 = jnp.max(lg, axis=1, keepdims=True)
    i1 = jnp.min(jnp.where(lg == m1, lane, big), axis=1, keepdims=True)
    rest = jnp.where(lane == i1, neg, lg)
    m2 = jnp.max(rest, axis=1, keepdims=True)
    i2 = jnp.min(jnp.where(rest == m2, lane, big), axis=1, keepdims=True)
    e2 = jnp.exp(m2 - m1)
    g1 = 1.0 / (1.0 + e2)
    g2 = e2 / (1.0 + e2)
    rt_ref[...] = (jnp.where(lane == 0, i1.astype(F32), 0.0) + jnp.where(lane == 1, i2.astype(F32), 0.0)
                   + jnp.where(lane == 2, g1, 0.0) + jnp.where(lane == 3, g2, 0.0))


def route_top2(logits, n_exp, *, tm=512):
    m, w = logits.shape
    tm = _tile(m, tm)
    spec = pl.BlockSpec((tm, w), lambda i: (i, 0))
    return pl.pallas_call(
        functools.partial(_route_kernel, n_exp=n_exp),
        grid=(m // tm,), in_specs=[spec], out_specs=spec,
        out_shape=jax.ShapeDtypeStruct((m, w), F32),
        compiler_params=_cp("parallel"),
        name="route_top2",
    )(logits)


def _routing_tables(route, n_exp, tm):
    m = route.shape[0]
    half = tm // 2
    e = jnp.concatenate([route[:, 0], route[:, 1]]).astype(jnp.int32)
    onehot = (e[:, None] == jnp.arange(n_exp, dtype=jnp.int32)[None, :]).astype(jnp.int32)
    csum = jnp.cumsum(onehot, axis=0)
    rank = jnp.take_along_axis(csum, e[:, None], axis=1)[:, 0] - 1
    counts = csum[-1]
    tiles_per = (counts + tm - 1) // tm
    tile_end = jnp.cumsum(tiles_per)
    tile_start = tile_end - tiles_per
    n_used = tile_end[-1]
    pos = tile_start[e] * tm + rank
    n_tiles = (TOP_K * m + tm - 1) // tm + n_exp
    tok = jnp.concatenate([jnp.arange(m, dtype=jnp.int32)] * TOP_K)
    src = jnp.zeros((n_tiles * tm,), jnp.int32).at[pos].set(tok)
    tile_ids = jnp.minimum(jnp.arange(n_tiles, dtype=jnp.int32), n_used - 1)
    tile_exp = jnp.minimum(jnp.searchsorted(tile_end, tile_ids, side='right'), n_exp - 1).astype(jnp.int32)
    rows_here = counts[tile_exp] - (tile_ids - tile_start[tile_exp]) * tm
    n_live = jnp.where(rows_here > half, tm, half).astype(jnp.int32)
    return src, tile_exp, n_live, n_used.reshape(1).astype(jnp.int32), pos.astype(jnp.int32), n_tiles


def _moe_expert_kernel(src_ref, texp_ref, nlive_ref, nused_ref, x_hbm, g_ref, wg_ref, wu_ref, wd_ref, o_ref,
                       xbuf, h_ref, sem, *, tm):
    t, j = pl.program_id(0), pl.program_id(1)
    n_used = nused_ref[0]
    slot = t % 2
    half = tm // 2
    group = 8

    def row_copy(tile, slot_, i):
        r = src_ref[tile * tm + i]
        return pltpu.make_async_copy(x_hbm.at[pl.ds(r, 1)], xbuf.at[slot_, pl.ds(i, 1)], sem.at[slot_])

    def start_tile(tile, slot_):
        def body(b, c):
            for k in range(group):
                row_copy(tile, slot_, b * group + k).start()
            return c
        lax.fori_loop(0, nlive_ref[tile] // group, body, 0)

    def wait_tile(tile, slot_):
        def body(b, c):
            for k in range(group):
                row_copy(tile, slot_, b * group + k).wait()
            return c
        lax.fori_loop(0, nlive_ref[tile] // group, body, 0)

    live = t < n_used
    full = nlive_ref[t] > half

    @pl.when((j == 0) & live)
    def _():
        @pl.when(t == 0)
        def _():
            start_tile(0, 0)

        @pl.when(t + 1 < n_used)
        def _():
            start_tile(t + 1, 1 - slot)

        wait_tile(t, slot)

        o_ref[...] = jnp.zeros(o_ref.shape, F32)

        @pl.when(full)
        def _():
            h_ref[...] = _rmsnorm(xbuf[slot], g_ref[...]).astype(BF16)

        @pl.when(jnp.logical_not(full))
        def _():
            h_ref[0:half, :] = _rmsnorm(xbuf[slot, 0:half, :], g_ref[...]).astype(BF16)

    def evaluate(rows):
        h = h_ref[0:rows, :]
        a = _silu(_dot(h, wg_ref[0].astype(BF16))) * _dot(h, wu_ref[0].astype(BF16))
        o_ref[0:rows, :] += _dot(a.astype(BF16), wd_ref[0].astype(BF16))

    @pl.when(live & full)
    def _():
        evaluate(tm)

    @pl.when(live & jnp.logical_not(full))
    def _():
        evaluate(half)

    @pl.when(jnp.logical_not(live) & (j == 0))
    def _():
        o_ref[...] = jnp.zeros(o_ref.shape, F32)


def moe_experts(x, g, src, tile_exp, n_live, n_used, n_tiles, wg, wu, wd, *, tm, tf=256):
    m, d = x.shape
    f = wg.shape[2]
    tf = _tile(f, tf, LANES)
    nj = f // tf

    def widx(t, j, src_r, texp_r, nlive_r, nused_r):
        return (texp_r[t], 0, jnp.where(t < nused_r[0], j, nj - 1))

    def didx(t, j, src_r, texp_r, nlive_r, nused_r):
        return (texp_r[t], jnp.where(t < nused_r[0], j, nj - 1), 0)

    grid_spec = pltpu.PrefetchScalarGridSpec(
        num_scalar_prefetch=4,
        grid=(n_tiles, nj),
        in_specs=[pl.BlockSpec(memory_space=pl.ANY),
                  pl.BlockSpec((1, d), lambda t, j, *_: (0, 0)),
                  pl.BlockSpec((1, d, tf), widx),
                  pl.BlockSpec((1, d, tf), widx),
                  pl.BlockSpec((1, tf, d), didx)],
        out_specs=pl.BlockSpec((tm, d), lambda t, j, *_: (t, 0)),
        scratch_shapes=[pltpu.VMEM((2, tm, d), F32), pltpu.VMEM((tm, d), BF16), pltpu.SemaphoreType.DMA((2,))],
    )
    return pl.pallas_call(
        functools.partial(_moe_expert_kernel, tm=tm),
        grid_spec=grid_spec,
        out_shape=jax.ShapeDtypeStruct((n_tiles * tm, d), F32),
        compiler_params=_cp("arbitrary", "arbitrary"),
        name="moe_experts",
    )(src, tile_exp, n_live, n_used, x, g.reshape(1, d), wg, wu, wd)


def _moe_combine_kernel(pos_ref, xp_ref, xs_ref, rt_ref, gf_ref, ys_hbm, op_ref, os_ref, ybuf, sem,
                        *, tc, m, n_p_tiles):
    i = pl.program_id(0)
    n = pl.num_programs(0)
    slot = i % 2

    def row_copy(tile, slot_, k, r):
        p = pos_ref[k * m + tile * tc + r]
        return pltpu.make_async_copy(ys_hbm.at[pl.ds(p, 1)], ybuf.at[slot_, k, pl.ds(r, 1)], sem.at[slot_])

    def start_tile(tile, slot_):
        def body(r, c):
            for k in range(TOP_K):
                row_copy(tile, slot_, k, r).start()
            return c
        lax.fori_loop(0, tc, body, 0, unroll=4)

    def wait_tile(tile, slot_):
        def body(r, c):
            for k in range(TOP_K):
                row_copy(tile, slot_, k, r).wait()
            return c
        lax.fori_loop(0, tc, body, 0, unroll=4)

    @pl.when(i == 0)
    def _():
        start_tile(0, 0)

    @pl.when(i + 1 < n)
    def _():
        start_tile(i + 1, 1 - slot)

    wait_tile(i, slot)
    rt = rt_ref[...]
    mix = rt[:, 2:3] * ybuf[slot, 0] + rt[:, 3:4] * ybuf[slot, 1]

    @pl.when(i < n_p_tiles)
    def _():
        op_ref[...] = _rmsnorm(xp_ref[...] + mix, gf_ref[...])

    @pl.when(i >= n_p_tiles)
    def _():
        os_ref[...] = _rmsnorm(xs_ref[...] + mix, gf_ref[...])


def moe_combine(x_p, x_s, route, pos, ys, g_final, *, tc=128):
    mp, d = x_p.shape
    ms = x_s.shape[0]
    tc = _tile(ms, tc)
    assert mp % tc == 0
    n_p_tiles, n_s_tiles = mp // tc, ms // tc
    m = mp + ms
    grid_spec = pltpu.PrefetchScalarGridSpec(
        num_scalar_prefetch=1,
        grid=(n_p_tiles + n_s_tiles,),
        in_specs=[pl.BlockSpec((tc, d), lambda i, *_: (jnp.minimum(i, n_p_tiles - 1), 0)),
                  pl.BlockSpec((tc, d), lambda i, *_: (jnp.maximum(i - n_p_tiles, 0), 0)),
                  pl.BlockSpec((tc, route.shape[1]), lambda i, *_: (i, 0)),
                  pl.BlockSpec((1, d), lambda i, *_: (0, 0)),
                  pl.BlockSpec(memory_space=pl.ANY)],
        out_specs=[pl.BlockSpec((tc, d), lambda i, *_: (jnp.minimum(i, n_p_tiles - 1), 0)),
                   pl.BlockSpec((tc, d), lambda i, *_: (jnp.maximum(i - n_p_tiles, 0), 0))],
        scratch_shapes=[pltpu.VMEM((2, TOP_K, tc, d), F32), pltpu.SemaphoreType.DMA((2,))],
    )
    return pl.pallas_call(
        functools.partial(_moe_combine_kernel, tc=tc, m=m, n_p_tiles=n_p_tiles),
        grid_spec=grid_spec,
        out_shape=[jax.ShapeDtypeStruct((mp, d), F32), jax.ShapeDtypeStruct((ms, d), F32)],
        compiler_params=_cp("arbitrary"),
        name="moe_combine",
    )(pos, x_p, x_s, route, g_final.reshape(1, d), ys)


def moe_top2(x_p, x_s, p, *, tm=1024):
    n_exp = p['w_router'].shape[1]
    x_all = jnp.concatenate([x_p, x_s], axis=0)
    logits = norm_matmul(x_all, p['norm_ffn'][1], _pad_cols(p['w_router']), exact=True)
    route = route_top2(logits, n_exp)
    src, tile_exp, n_live, n_used, pos, n_tiles = _routing_tables(route, n_exp, tm)
    ys = moe_experts(x_all, p['norm_ffn'][1], src, tile_exp, n_live, n_used, n_tiles,
                     p['w_exp_gate'], p['w_exp_up'], p['w_exp_down'], tm=tm)
    return moe_combine(x_p, x_s, route, pos, ys, p['norm_final'])


def _pad_cols(w, width=LANES):
    return jnp.pad(w, ((0, 0), (0, width - w.shape[1])))


def _split_in1(p, c_width, nh):
    w_main = jnp.concatenate([p['w_in1'][:, :3 * c_width], p['w_in1'][:, 3 * c_width + nh:]], axis=1)
    w_fl = _pad_cols(p['w_in1'][:, 3 * c_width:3 * c_width + nh])
    return w_main, w_fl


def trunk_prompt(x3, mem, p):
    n_seq, t, d = x3.shape
    x = x3.reshape(n_seq * t, d)
    a_width = p['w_conv_a'].shape[1]
    nh = p['b_forget'].shape[0]
    c_width = nh * HEAD_DIM
    n_mem = mem.shape[1]
    xw = p['w_xk'].shape[2]

    mem2 = mem.reshape(n_seq * n_mem, d)
    mk, mv = [], []
    for layer in range(2):
        mk.append(norm_matmul(mem2, p['norm_mem'][layer], p['w_xk'][layer]).reshape(n_seq, n_mem, xw))
        mv.append(norm_matmul(mem2, p['norm_mem'][layer], p['w_xv'][layer]).reshape(n_seq, n_mem, xw))

    def cross(x, layer):
        q = norm_matmul(x, p['norm_cross'][layer], p['w_xq'][layer])
        return matmul_residual([xattn_prompt(q, mk[layer], mv[layer], n_seq, t)], p['w_xo'][layer], x)

    z = norm_matmul(x, p['norm_mix'][0], p['w_in0'])
    ya, h_last, xr_last = lru_prompt(z, n_seq, t, p['w_conv_a'], p['b_conv_a'], p['w_rg_a'], p['b_rg_a'],
                                     p['w_rg_i'], p['b_rg_i'], p['lru_lambda'])
    yb = sgu_prompt(z, n_seq, t, a_width, p['ln_v_g'], p['ln_v_b'], p['w_spatial'], p['b_spatial'])
    x = matmul_residual([ya, yb], p['w_out0'], x)
    x = cross(x, 0)
    x = ffn_swiglu(x, p['norm_ffn'][0], p['w_ffn_gate'], p['w_ffn_up'], p['w_ffn_down'])

    w_main, w_fl = _split_in1(p, c_width, nh)
    zz = norm_matmul(x, p['norm_mix'][1], w_main)
    qkv = zz
    fl = norm_matmul(x, p['norm_mix'][1], w_fl)[:, :nh]
    fl_t = fl.reshape(n_seq, t, nh).transpose(0, 2, 1)
    lf_t, c_t = logf_cumsum(fl_t, p['b_forget'])
    yc = fox_prompt(zz, c_t, n_seq, t, nh)
    yd, glu_tail = convd_prompt(zz, 3 * c_width, n_seq, t, p['w_conv_d'], p['b_conv_d'], p['ln_d_g'], p['ln_d_b'])
    x = matmul_residual([yc, yd], p['w_out1'], x)
    x = cross(x, 1)

    width_a = p['w_conv_a'].shape[0]
    width_d = p['w_conv_d'].shape[0]
    return x, dict(
        lru_h=h_last[:, 7, :],
        lru_conv=xr_last[:, 8 - (width_a - 1):, :],
        k=qkv[:, c_width:2 * c_width].reshape(n_seq, t, nh, HEAD_DIM),
        v=qkv[:, 2 * c_width:3 * c_width].reshape(n_seq, t, nh, HEAD_DIM),
        logf=lf_t.transpose(0, 2, 1),
        conv_d=glu_tail[:, 32 - (width_d - 1):, :],
        mem_k=jnp.stack(mk).reshape(2, n_seq, n_mem, xw // HEAD_DIM, HEAD_DIM),
        mem_v=jnp.stack(mv).reshape(2, n_seq, n_mem, xw // HEAD_DIM, HEAD_DIM),
    )


def trunk_sample(x3, mem_k, mem_v, lru_h0, lru_conv0, conv_d0, cache_k, cache_v, cache_lf, page_table, p):
    n, t, d = x3.shape
    assert t == 1
    x = x3.reshape(n, d)
    nh = p['b_forget'].shape[0]
    c_width = nh * HEAD_DIM
    xh = mem_k.shape[3]

    def cross(x, layer):
        q = norm_matmul(x, p['norm_cross'][layer], p['w_xq'][layer], exact=True).reshape(n, xh, HEAD_DIM)
        o = xattn_sample(q, mem_k, mem_v, layer).reshape(n, xh * HEAD_DIM)
        return matmul_residual([o], p['w_xo'][layer], x, exact=True)

    z = norm_matmul(x, p['norm_mix'][0], p['w_in0'], exact=True)
    a_width = lru_h0.shape[1]
    yab, lru_h, chunk_v = mixer0_sample(z, lru_h0, lru_conv0, p['w_conv_a'], p['b_conv_a'], p['w_rg_a'], p['b_rg_a'],
                                        p['w_rg_i'], p['b_rg_i'], p['lru_lambda'], p['ln_v_g'], p['ln_v_b'],
                                        p['w_spatial'], p['b_spatial'])
    x = matmul_residual([yab], p['w_out0'], x, exact=True)
    x = cross(x, 0)
    x = ffn_swiglu(x, p['norm_ffn'][0], p['w_ffn_gate'], p['w_ffn_up'], p['w_ffn_down'], exact=True)

    w_main, w_fl = _split_in1(p, c_width, nh)
    zz = norm_matmul(x, p['norm_mix'][1], w_main, exact=True)
    q, k, v = (zz[:, i * c_width:(i + 1) * c_width].reshape(n, nh, HEAD_DIM) for i in range(3))
    gd = zz[:, 3 * c_width:]
    fl = norm_matmul(x, p['norm_mix'][1], w_fl, exact=True)[:, :nh]
    lf_t, _ = logf_cumsum(fl.T.reshape(1, nh, n), p['b_forget'])
    lf = lf_t.reshape(nh, n).T
    yc = fox_sample(page_table, q, k, v, lf, cache_k, cache_v, cache_lf.transpose(0, 2, 1))
    yd, glu = convd_sample(gd, conv_d0, p['w_conv_d'], p['b_conv_d'], p['ln_d_g'], p['ln_d_b'])
    x = matmul_residual([yc.reshape(n, c_width), yd], p['w_out1'], x, exact=True)
    x = cross(x, 1)

    xr = z[:, a_width:2 * a_width]
    return x, dict(
        lru_h=lru_h,
        lru_conv=jnp.concatenate([lru_conv0[:, 1:], xr[:, None, :]], axis=1),
        chunk_v=chunk_v.reshape(n, 1, -1),
        k=k.reshape(n, 1, nh, HEAD_DIM),
        v=v.reshape(n, 1, nh, HEAD_DIM),
        logf=lf.reshape(n, 1, nh),
        conv_d=jnp.concatenate([conv_d0[:, 1:], glu[:, None, :]], axis=1),
    )


def kernel(x_prompt, x_sample, mem_prompt, state_lru_h, state_lru_conv, cache_fox_k, cache_fox_v, cache_fox_logf, state_conv_d, cache_mem_k, cache_mem_v, page_table, norm_mix, norm_cross, norm_mem, norm_ffn, norm_final, w_in0, w_conv_a, b_conv_a, w_rg_a, b_rg_a, w_rg_i, b_rg_i, lru_lambda, ln_v_g, ln_v_b, w_spatial, b_spatial, w_out0, w_in1, b_forget, w_conv_d, b_conv_d, ln_d_g, ln_d_b, w_out1, w_xq, w_xk, w_xv, w_xo, w_ffn_gate, w_ffn_up, w_ffn_down, w_router, w_exp_gate, w_exp_up, w_exp_down):
    p = dict(norm_mix=norm_mix, norm_cross=norm_cross, norm_mem=norm_mem, norm_ffn=norm_ffn, norm_final=norm_final,
             w_in0=w_in0, w_conv_a=w_conv_a, b_conv_a=b_conv_a, w_rg_a=w_rg_a, b_rg_a=b_rg_a,
             w_rg_i=w_rg_i, b_rg_i=b_rg_i, lru_lambda=lru_lambda, ln_v_g=ln_v_g, ln_v_b=ln_v_b,
             w_spatial=w_spatial, b_spatial=b_spatial, w_out0=w_out0,
             w_in1=w_in1, b_forget=b_forget, w_conv_d=w_conv_d, b_conv_d=b_conv_d,
             ln_d_g=ln_d_g, ln_d_b=ln_d_b, w_out1=w_out1, w_xq=w_xq, w_xk=w_xk, w_xv=w_xv, w_xo=w_xo,
             w_ffn_gate=w_ffn_gate, w_ffn_up=w_ffn_up, w_ffn_down=w_ffn_down,
             w_router=w_router, w_exp_gate=w_exp_gate, w_exp_up=w_exp_up, w_exp_down=w_exp_down)
    x_p, pr = trunk_prompt(x_prompt, mem_prompt, p)
    x_s, sm = trunk_sample(x_sample, cache_mem_k, cache_mem_v, state_lru_h, state_lru_conv, state_conv_d,
                           cache_fox_k, cache_fox_v, cache_fox_logf, page_table, p)
    y_p, y_s = moe_top2(x_p, x_s, p)
    return (y_p.reshape(x_prompt.shape), y_s.reshape(x_sample.shape),
            pr['lru_h'], pr['lru_conv'], pr['k'], pr['v'], pr['logf'], pr['conv_d'], pr['mem_k'], pr['mem_v'],
            sm['lru_h'], sm['lru_conv'], sm['chunk_v'], sm['k'], sm['v'], sm['logf'], sm['conv_d'])
```
